```python
import jax, jax.numpy as jnp
from jax import lax
import numpy as np

D_MODEL = 2048
BATCH = 2
SEQ = 16384
DEPTH = 2

CHUNK = 64
N_MEM = 256
EPS = 1e-6
D_FF = 256 * ((8 * D_MODEL // 3 + 255) // 256)
CONV_K = 4
BRANCH_W = D_MODEL // 2
N_BRANCH = 3

ML_HEADS = 4
ML_V = BRANCH_W // ML_HEADS
ML_QK = ML_V // 2
ML_QK_W = ML_HEADS * ML_QK
ML_V_W = ML_HEADS * ML_V

DSA_HEADS = 8
DSA_HD = BRANCH_W // DSA_HEADS
DSA_LAT = D_MODEL // 4
IDX_HEADS = 16
IDX_D = 64
INDEX_TOPK = 256
Q_BLOCK = 128

SSM_P = 64
SSM_HEADS = BRANCH_W // SSM_P
SSM_G = 2
SSM_N = 128
SSM_XBC = BRANCH_W + 2 * SSM_G * SSM_N

XA_HEADS = 4
XA_HD = 128
XA_W = XA_HEADS * XA_HD

IN_SIZES = (ML_QK_W, ML_QK_W, ML_V_W, ML_V_W, ML_HEADS, ML_HEADS,
            BRANCH_W, DSA_LAT, IDX_HEADS * IDX_D, IDX_D, IDX_HEADS,
            BRANCH_W, SSM_XBC, SSM_HEADS,
            N_BRANCH * D_MODEL)
IN_SPLITS = tuple(int(s) for s in np.cumsum(IN_SIZES)[:-1])
IN_W = sum(IN_SIZES)

kernel_name = 'hybrid_mlstm_dsa_ssd_macaron'


def rmsnorm(x, g):
    xf = x.astype(jnp.float32)
    y = xf * lax.rsqrt(jnp.mean(xf * xf, axis=-1, keepdims=True) + EPS)
    return (y * g.astype(jnp.float32)).astype(x.dtype)


def swiglu(h, w_up, w_down):
    a, b = jnp.split(h @ w_up, 2, axis=-1)
    return (jax.nn.silu(a) * b) @ w_down


def causal_dwconv(x, w):
    return lax.conv_general_dilated(
        x, w[:, None, :].astype(x.dtype), (1,), [(w.shape[0] - 1, 0)],
        dimension_numbers=('NWC', 'WIO', 'NWC'), feature_group_count=x.shape[-1])


def to_chunks(a):
    return a.reshape((a.shape[0], a.shape[1] // CHUNK, CHUNK) + a.shape[2:]).swapaxes(0, 1)


def from_chunks(a):
    return a.swapaxes(0, 1).reshape((a.shape[1], a.shape[0] * a.shape[2]) + a.shape[3:])


def mlstm_chunkwise(q, k, v, i_pre, f_pre):
    bsz, _, nh, dk = q.shape
    dv = v.shape[-1]
    causal = jnp.tril(jnp.ones((CHUNK, CHUNK), bool))[None, :, :, None]
    b_cum = jnp.cumsum(to_chunks(jax.nn.log_sigmoid(f_pre)), axis=2)

    def step(carry, inp):
        C, n, m = carry
        qc, kc, vc, ic, bc = inp
        dmat = bc[:, :, None, :] - bc[:, None, :, :] + ic[:, None, :, :]
        dmat = jnp.where(causal, dmat, -jnp.inf)
        inter = bc + m[:, None, :]
        m_t = jnp.maximum(inter, dmat.max(axis=2))
        s = jnp.einsum('blhd,bshd->blsh', qc, kc) * jnp.exp(dmat - m_t[:, :, None, :])
        decay = jnp.exp(inter - m_t)
        num = jnp.einsum('blsh,bshv->blhv', s, vc) + decay[..., None] * jnp.einsum('bhvd,blhd->blhv', C, qc)
        den = s.sum(axis=2) + decay * jnp.einsum('bhd,blhd->blh', n, qc)
        h = num / jnp.maximum(jnp.abs(den), jnp.exp(-m_t))[..., None]
        b_end = bc[:, -1]
        g = b_end[:, None, :] - bc + ic
        m_new = jnp.maximum(b_end + m, g.max(axis=1))
        wk = jnp.exp(g - m_new[:, None, :])
        sdec = jnp.exp(b_end + m - m_new)
        C_new = sdec[..., None, None] * C + jnp.einsum('blh,blhv,blhd->bhvd', wk, vc, kc)
        n_new = sdec[..., None] * n + jnp.einsum('blh,blhd->bhd', wk, kc)
        return (C_new, n_new, m_new), h

    init = (jnp.zeros((bsz, nh, dv, dk), jnp.float32),
            jnp.zeros((bsz, nh, dk), jnp.float32),
            jnp.zeros((bsz, nh), jnp.float32))
    _, h = lax.scan(step, init, (to_chunks(q), to_chunks(k), to_chunks(v), to_chunks(i_pre), b_cum))
    return from_chunks(h)


def ssd_chunked(xs, Bm, Cm, dt, A):
    bsz, _, nh, p = xs.shape
    rep = nh // Bm.shape[2]
    n = Bm.shape[-1]
    causal = jnp.tril(jnp.ones((CHUNK, CHUNK), bool))[None, :, :, None]
    seg_all = jnp.cumsum(to_chunks(dt * A), axis=2)

    def step(state, inp):
        xc, bc, cc, dtc, seg = inp
        Bh = jnp.repeat(bc, rep, axis=2)
        Ch = jnp.repeat(cc, rep, axis=2)
        diff = seg[:, :, None, :] - seg[:, None, :, :]
        lmat = jnp.exp(jnp.where(causal, diff, -jnp.inf))
        scores = jnp.einsum('blhn,bshn->blsh', Ch, Bh) * lmat * dtc[:, None, :, :]
        y = (jnp.einsum('blsh,bshp->blhp', scores, xc)
             + jnp.einsum('blhn,bhpn->blhp', Ch, state) * jnp.exp(seg)[..., None])
        w_end = jnp.exp(seg[:, -1:, :] - seg) * dtc
        state = (jnp.exp(seg[:, -1])[:, :, None, None] * state
                 + jnp.einsum('bsh,bshn,bshp->bhpn', w_end, Bh, xc))
        return state, y

    init = jnp.zeros((bsz, nh, p, n), jnp.float32)
    _, y = lax.scan(step, init, (to_chunks(xs), to_chunks(Bm), to_chunks(Cm), to_chunks(dt), seg_all))
    return from_chunks(y)


def dsa_attention(q, k, v, q_idx, k_idx, w_idx):
    bsz, s_len, nh, dh = q.shape
    top = min(INDEX_TOPK, s_len // 4)
    key_pos = jnp.arange(s_len)

    def block(start):
        qb = lax.dynamic_slice_in_dim(q, start, Q_BLOCK, axis=1)
        qib = lax.dynamic_slice_in_dim(q_idx, start, Q_BLOCK, axis=1)
        wib = lax.dynamic_slice_in_dim(w_idx, start, Q_BLOCK, axis=1)
        pos = start + jnp.arange(Q_BLOCK)
        limit = (pos // CHUNK + 1) * CHUNK
        admissible = key_pos[None, :] < limit[:, None]
        raw = jnp.einsum('bqhd,bsd->bqhs', qib, k_idx)
        score = jnp.einsum('bqhs,bqh->bqs', jax.nn.relu(raw), wib).astype(jnp.float32)
        score = jnp.where(admissible[None], score, -jnp.inf)
        _, idx = lax.top_k(score, top)
        valid = idx < limit[None, :, None]
        k_sel = jax.vmap(lambda kb, ib: kb[ib])(k, idx)
        v_sel = jax.vmap(lambda vb, ib: vb[ib])(v, idx)
        logits = jnp.einsum('bqhd,bqjhd->bhqj', qb, k_sel).astype(jnp.float32)
        logits = jnp.where(valid[:, None], logits, -jnp.inf)
        p = jax.nn.softmax(logits, axis=-1).astype(v.dtype)
        return jnp.einsum('bhqj,bqjhd->bqhd', p, v_sel)

    out = lax.map(block, jnp.arange(s_len // Q_BLOCK) * Q_BLOCK)
    return out.swapaxes(0, 1).reshape(bsz, s_len, nh * dh)


def hybrid_mixer(h, w_in, ml_conv, ml_i_bias, ml_f_bias, ml_out_norm,
                 dsa_q_norm, dsa_k_norm, dsa_kv_norm, dsa_w_uk, dsa_w_uv, idx_k_norm,
                 ssm_conv, ssm_conv_b, ssm_dt_bias, ssm_a_log, ssm_d, ssm_norm,
                 w_branch, w_out):
    bsz, s, _ = h.shape
    f32 = jnp.float32
    (ml_q, ml_k, ml_v, ml_o, ml_i, ml_f,
     d_q, d_kv, i_q, i_k, i_w,
     s_z, s_xbc, s_dt, gates) = jnp.split(h @ w_in, IN_SPLITS, axis=-1)

    qk = jax.nn.silu(causal_dwconv(jnp.concatenate([ml_q, ml_k], axis=-1), ml_conv))
    q_m = qk[..., :ML_QK_W].reshape(bsz, s, ML_HEADS, ML_QK).astype(f32) * ML_QK ** -0.5
    k_m = qk[..., ML_QK_W:].reshape(bsz, s, ML_HEADS, ML_QK).astype(f32)
    v_m = ml_v.reshape(bsz, s, ML_HEADS, ML_V).astype(f32)
    h_m = mlstm_chunkwise(q_m, k_m, v_m, (ml_i + ml_i_bias).astype(f32), (ml_f + ml_f_bias).astype(f32))
    h_m = rmsnorm(h_m, ml_out_norm.reshape(ML_HEADS, ML_V)).reshape(bsz, s, ML_V_W).astype(h.dtype)
    y_a = jax.nn.sigmoid(ml_o) * h_m

    q_d = rmsnorm(d_q.reshape(bsz, s, DSA_HEADS, DSA_HD), dsa_q_norm) * DSA_HD ** -0.5
    c_kv = rmsnorm(d_kv, dsa_kv_norm)
    k_d = rmsnorm((c_kv @ dsa_w_uk).reshape(bsz, s, DSA_HEADS, DSA_HD), dsa_k_norm)
    v_d = (c_kv @ dsa_w_uv).reshape(bsz, s, DSA_HEADS, DSA_HD)
    y_b = dsa_attention(q_d, k_d, v_d,
                        i_q.reshape(bsz, s, IDX_HEADS, IDX_D) * IDX_D ** -0.5,
                        rmsnorm(i_k, idx_k_norm),
                        i_w * IDX_HEADS ** -0.5)

    xbc = jax.nn.silu(causal_dwconv(s_xbc, ssm_conv) + ssm_conv_b)
    x_s = xbc[..., :BRANCH_W].reshape(bsz, s, SSM_HEADS, SSM_P).astype(f32)
    b_s = xbc[..., BRANCH_W:BRANCH_W + SSM_G * SSM_N].reshape(bsz, s, SSM_G, SSM_N).astype(f32)
    c_s = xbc[..., BRANCH_W + SSM_G * SSM_N:].reshape(bsz, s, SSM_G, SSM_N).astype(f32)
    dt = jax.nn.softplus((s_dt + ssm_dt_bias).astype(f32))
    a = -jnp.exp(ssm_a_log.astype(f32))
    y_s = ssd_chunked(x_s, b_s, c_s, dt, a) + ssm_d.astype(f32)[:, None] * x_s
    y_c = rmsnorm(y_s.reshape(bsz, s, BRANCH_W).astype(h.dtype) * jax.nn.silu(s_z), ssm_norm)

    g = jax.nn.sigmoid(gates).reshape(bsz, s, N_BRANCH, D_MODEL)
    merged = (g[:, :, 0] * (y_a @ w_branch[0])
              + g[:, :, 1] * (y_b @ w_branch[1])
              + g[:, :, 2] * (y_c @ w_branch[2]))
    return merged @ w_out


def memory_cross_attention(h, m, wq, wkv, q_norm, k_norm, wo):
    bsz, s, _ = h.shape
    n_mem = m.shape[1]
    q = rmsnorm((h @ wq).reshape(bsz, s, XA_HEADS, XA_HD), q_norm) * XA_HD ** -0.5
    kv = m @ wkv
    k = rmsnorm(kv[..., :XA_W].reshape(bsz, n_mem, XA_HEADS, XA_HD), k_norm)
    v = kv[..., XA_W:].reshape(bsz, n_mem, XA_HEADS, XA_HD)
    p = jax.nn.softmax(jnp.einsum('bshd,bmhd->bhsm', q, k).astype(jnp.float32), axis=-1).astype(v.dtype)
    o = jnp.einsum('bhsm,bmhd->bshd', p, v).reshape(bsz, s, XA_W)
    return o @ wo


def setup_inputs(seed: int = 0) -> dict:
    key = jax.random.key(seed)
    keys = iter(jax.random.split(key, 64))
    f32 = jnp.float32
    L = (DEPTH,)

    def nrm(shape, scale):
        return scale * jax.random.normal(next(keys), shape, f32)

    def gain(shape):
        return 1.0 + 0.05 * jax.random.normal(next(keys), shape, f32)

    dt0 = jnp.exp(jax.random.uniform(next(keys), L + (SSM_HEADS,), f32,
                                     np.log(1e-3).astype(np.float32), np.log(1e-1).astype(np.float32)))
    return {
        'x': nrm((BATCH, SEQ, D_MODEL), 1.0),
        'mem': nrm((BATCH, N_MEM, D_MODEL), 1.0),
        'ffn1_norm': gain(L + (D_MODEL,)),
        'ffn1_w_up': nrm(L + (D_MODEL, 2 * D_FF), D_MODEL ** -0.5),
        'ffn1_w_down': nrm(L + (D_FF, D_MODEL), D_FF ** -0.5),
        'mix_norm': gain(L + (D_MODEL,)),
        'w_in': nrm(L + (D_MODEL, IN_W), D_MODEL ** -0.5),
        'ml_conv': nrm(L + (CONV_K, 2 * ML_QK_W), CONV_K ** -0.5),
        'ml_i_bias': nrm(L + (ML_HEADS,), 0.1),
        'ml_f_bias': 3.0 + nrm(L + (ML_HEADS,), 0.5),
        'ml_out_norm': gain(L + (ML_V_W,)),
        'dsa_q_norm': gain(L + (DSA_HD,)),
        'dsa_k_norm': gain(L + (DSA_HD,)),
        'dsa_kv_norm': gain(L + (DSA_LAT,)),
        'dsa_w_uk': nrm(L + (DSA_LAT, BRANCH_W), DSA_LAT ** -0.5),
        'dsa_w_uv': nrm(L + (DSA_LAT, BRANCH_W), DSA_LAT ** -0.5),
        'idx_k_norm': gain(L + (IDX_D,)),
        'ssm_conv': nrm(L + (CONV_K, SSM_XBC), CONV_K ** -0.5),
        'ssm_conv_b': nrm(L + (SSM_XBC,), 0.02),
        'ssm_dt_bias': dt0 + jnp.log(-jnp.expm1(-dt0)),
        'ssm_a_log': jnp.log(jax.random.uniform(next(keys), L + (SSM_HEADS,), f32, 1.0, 16.0)),
        'ssm_d': gain(L + (SSM_HEADS,)),
        'ssm_norm': gain(L + (BRANCH_W,)),
        'w_branch': nrm(L + (N_BRANCH, BRANCH_W, D_MODEL), BRANCH_W ** -0.5),
        'w_out': nrm(L + (D_MODEL, D_MODEL), D_MODEL ** -0.5),
        'xa_norm': gain(L + (D_MODEL,)),
        'xa_mem_norm': gain(L + (D_MODEL,)),
        'xa_wq': nrm(L + (D_MODEL, XA_W), D_MODEL ** -0.5),
        'xa_wkv': nrm(L + (D_MODEL, 2 * XA_W), D_MODEL ** -0.5),
        'xa_q_norm': gain(L + (XA_HD,)),
        'xa_k_norm': gain(L + (XA_HD,)),
        'xa_wo': nrm(L + (XA_W, D_MODEL), XA_W ** -0.5),
        'ffn2_norm': gain(L + (D_MODEL,)),
        'ffn2_w_up': nrm(L + (D_MODEL, 2 * D_FF), D_MODEL ** -0.5),
        'ffn2_w_down': nrm(L + (D_FF, D_MODEL), D_FF ** -0.5),
    }


def reference(x, mem, ffn1_norm, ffn1_w_up, ffn1_w_down, mix_norm, w_in,
              ml_conv, ml_i_bias, ml_f_bias, ml_out_norm,
              dsa_q_norm, dsa_k_norm, dsa_kv_norm, dsa_w_uk, dsa_w_uv, idx_k_norm,
              ssm_conv, ssm_conv_b, ssm_dt_bias, ssm_a_log, ssm_d, ssm_norm,
              w_branch, w_out, xa_norm, xa_mem_norm, xa_wq, xa_wkv, xa_q_norm, xa_k_norm, xa_wo,
              ffn2_norm, ffn2_w_up, ffn2_w_down):
    for l in range(DEPTH):
        x = x + 0.5 * swiglu(rmsnorm(x, ffn1_norm[l]), ffn1_w_up[l], ffn1_w_down[l])
        x = x + hybrid_mixer(rmsnorm(x, mix_norm[l]), w_in[l],
                             ml_conv[l], ml_i_bias[l], ml_f_bias[l], ml_out_norm[l],
                             dsa_q_norm[l], dsa_k_norm[l], dsa_kv_norm[l], dsa_w_uk[l], dsa_w_uv[l], idx_k_norm[l],
                             ssm_conv[l], ssm_conv_b[l], ssm_dt_bias[l], ssm_a_log[l], ssm_d[l], ssm_norm[l],
                             w_branch[l], w_out[l])
        x = x + memory_cross_attention(rmsnorm(x, xa_norm[l]), rmsnorm(mem, xa_mem_norm[l]),
                                       xa_wq[l], xa_wkv[l], xa_q_norm[l], xa_k_norm[l], xa_wo[l])
        x = x + 0.5 * swiglu(rmsnorm(x, ffn2_norm[l]), ffn2_w_up[l], ffn2_w_down[l])
    return x
```

```python
import functools

import numpy as np
import jax
import jax.numpy as jnp
from jax import lax
from jax.experimental import pallas as pl
from jax.experimental.pallas import tpu as pltpu

F32 = jnp.float32
BF16 = jnp.bfloat16
I32 = jnp.int32

EPS = 1e-6
CHUNK = 64
CONV_K = 4
LANES = 128
SUBLANES = 8
VMEM_LIMIT_BYTES = 56 * 1024 * 1024

ML_HEADS, ML_QK, ML_V = 4, 128, 256
DSA_HEADS, DSA_HD, DSA_LAT = 8, 128, 512
IDX_HEADS, IDX_D, INDEX_TOPK = 16, 64, 256
Q_BLOCK = 128
SSM_HEADS, SSM_P, SSM_G, SSM_N = 16, 64, 2, 128
SSM_HPG = SSM_HEADS // SSM_G
XA_HEADS, XA_HD = 4, 128
N_BRANCH = 3

SM_IK, SM_IW, SM_DT, SM_MI, SM_MF = 0, 64, 80, 96, 100

NEG_BIG = -1e30
INT_MIN = -2 ** 31
NEG_INF_KEY = -2139095041


def _cparams(sem):
    return pltpu.CompilerParams(dimension_semantics=sem, vmem_limit_bytes=VMEM_LIMIT_BYTES)


def _rms(x):
    return x * lax.rsqrt(jnp.mean(x * x, axis=-1, keepdims=True) + EPS)


def _sigmoid(x):
    return 1.0 / (1.0 + jnp.exp(-x))


def _silu(x):
    return x * _sigmoid(x)


def _softplus(x):
    return jnp.maximum(x, 0.0) + jnp.log(1.0 + jnp.exp(-jnp.abs(x)))


def _dot(a, b):
    return jnp.dot(a, b, preferred_element_type=F32)


def _dot_nt(a, b):
    return lax.dot_general(a, b, (((1,), (1,)), ((), ())), preferred_element_type=F32)


def _dot_tn(a, b):
    return lax.dot_general(a, b, (((0,), (0,)), ((), ())), preferred_element_type=F32)


def _split3(x):
    hi = x.astype(BF16)
    r = x - hi.astype(F32)
    mid = r.astype(BF16)
    lo = (r - mid.astype(F32)).astype(BF16)
    return hi, mid, lo


def _dot01_left(m01, x):
    hi, mid, lo = _split3(x)
    return _dot(m01, hi) + _dot(m01, mid) + _dot(m01, lo)


def _dot01_right(x, m01):
    hi, mid, lo = _split3(x)
    return _dot(hi, m01) + _dot(mid, m01) + _dot(lo, m01)


def _tri(n):
    r = lax.broadcasted_iota(I32, (n, n), 0)
    c = lax.broadcasted_iota(I32, (n, n), 1)
    return r >= c


def _t_blocks(x):
    cols = x.shape[1]
    return jnp.concatenate([x[:, c:c + LANES].T for c in range(0, cols, LANES)], axis=0)


def _ffn_body(x_ref, g_ref, wa_ref, wb_ref, wd_ref, o_ref, xn_ref):
    @pl.when(pl.program_id(1) == 0)
    def _():
        x = x_ref[...]
        xn_ref[...] = (_rms(x) * g_ref[...]).astype(BF16)
        o_ref[...] = x

    xn = xn_ref[...]
    a = _dot(xn, wa_ref[...])
    b = _dot(xn, wb_ref[...])
    act = (0.5 * _silu(a) * b).astype(BF16)
    o_ref[...] += _dot(act, wd_ref[...])


def _ffn(x2, g, w_up, w_down, tm=512, tf=512):
    m, d = x2.shape
    dff = w_down.shape[0]
    nf = dff // tf
    return pl.pallas_call(
        _ffn_body,
        grid=(m // tm, nf),
        in_specs=[
            pl.BlockSpec((tm, d), lambda i, f: (i, 0)),
            pl.BlockSpec((1, d), lambda i, f: (0, 0)),
            pl.BlockSpec((d, tf), lambda i, f: (0, f)),
            pl.BlockSpec((d, tf), lambda i, f: (0, nf + f)),
            pl.BlockSpec((tf, d), lambda i, f: (f, 0)),
        ],
        out_specs=pl.BlockSpec((tm, d), lambda i, f: (i, 0)),
        out_shape=jax.ShapeDtypeStruct((m, d), F32),
        scratch_shapes=[pltpu.VMEM((tm, d), BF16)],
        compiler_params=_cparams(("parallel", "arbitrary")),
    )(x2, g.reshape(1, d), w_up, w_up, w_down)


def _proj_body(x_ref, g_ref, w_ref, o_ref, xn_ref):
    @pl.when(pl.program_id(1) == 0)
    def _():
        xn_ref[...] = (_rms(x_ref[...]) * g_ref[...]).astype(BF16)

    o_ref[...] = _dot(xn_ref[...], w_ref[...])


def _proj(x2, g, w, tm=512, tn=512):
    m, d = x2.shape
    n = w.shape[1]
    tm = min(tm, m)
    return pl.pallas_call(
        _proj_body,
        grid=(m // tm, n // tn),
        in_specs=[
            pl.BlockSpec((tm, d), lambda i, j: (i, 0)),
            pl.BlockSpec((1, d), lambda i, j: (0, 0)),
            pl.BlockSpec((d, tn), lambda i, j: (0, j)),
        ],
        out_specs=pl.BlockSpec((tm, tn), lambda i, j: (i, j)),
        out_shape=jax.ShapeDtypeStruct((m, n), F32),
        scratch_shapes=[pltpu.VMEM((tm, d), BF16)],
        compiler_params=_cparams(("parallel", "arbitrary")),
    )(x2, g.reshape(1, d), w)


PJ_GATES = 0
PJ_MLQK = 6144
PJ_MLV = 7168
PJ_MLO = 8192
PJ_DQ = 9216
PJ_IQ = 10240
PJ_SZ = 11264
PJ_XBC = 12288
PJ_DKV = 13824
PJ_SMALL = 14336
PJ_WIDTH = 14848


def _reorder_w_in(w_in):
    d = w_in.shape[0]
    sizes = (512, 512, 1024, 1024, 4, 4, 1024, 512, 1024, 64, 16, 1024, 1536, 16, 6144)
    offs = np.concatenate([[0], np.cumsum(sizes)])
    (ml_q, ml_k, ml_v, ml_o, ml_i, ml_f, d_q, d_kv, i_q, i_k, i_w, s_z, s_xbc, s_dt, gates) = [
        w_in[:, int(offs[j]):int(offs[j + 1])] for j in range(len(sizes))]
    small = jnp.concatenate(
        [i_k, i_w, s_dt, ml_i, ml_f, jnp.zeros((d, LANES - 104), w_in.dtype)], axis=1)
    pad = jnp.zeros((d, PJ_WIDTH - PJ_SMALL - LANES), w_in.dtype)
    return jnp.concatenate(
        [gates, ml_q, ml_k, ml_v, ml_o, d_q, i_q, s_z, s_xbc, d_kv, small, pad], axis=1).astype(BF16)


def _causal_conv(x_ref, w_ref, carry_ref, xpad_ref, first):
    t_len = x_ref.shape[0]

    @pl.when(first)
    def _():
        carry_ref[...] = jnp.zeros_like(carry_ref)

    xpad_ref[0:SUBLANES, :] = carry_ref[...]
    xpad_ref[SUBLANES:SUBLANES + t_len, :] = x_ref[...]
    carry_ref[...] = x_ref[t_len - SUBLANES:t_len, :]
    acc = None
    for j in range(CONV_K):
        start = SUBLANES - (CONV_K - 1) + j
        term = w_ref[j:j + 1, :] * xpad_ref[start:start + t_len, :]
        acc = term if acc is None else acc + term
    return acc


def _mlstm_body(qk_ref, v_ref, o_ref, sm_ref, conv_ref, bias_ref, onorm_ref, y_ref,
                carry_ref, xpad_ref, q_s, k_s, ct_ref, n_ref, m_ref):
    first = pl.program_id(1) == 0
    t_len = qk_ref.shape[0]
    qkw = ML_HEADS * ML_QK

    @pl.when(first)
    def _():
        ct_ref[...] = jnp.zeros_like(ct_ref)
        n_ref[...] = jnp.zeros_like(n_ref)
        m_ref[...] = jnp.zeros_like(m_ref)

    qk = _silu(_causal_conv(qk_ref, conv_ref, carry_ref, xpad_ref, first))
    q_s[...] = (qk[:, :qkw] * (ML_QK ** -0.5)).astype(BF16)
    k_s[...] = qk[:, qkw:].astype(BF16)

    causal = _tri(CHUNK)
    tri01 = causal.astype(BF16)

    def chunk(c, carry):
        r0 = pl.multiple_of(c * CHUNK, CHUNK)
        rows = pl.ds(r0, CHUNK)
        sm = sm_ref[rows, :] + bias_ref[...]
        lf = -_softplus(-sm)
        b_col = _dot01_left(tri01, lf)
        b_t = b_col.T
        s_t = sm.T
        for h in range(ML_HEADS):
            bc = b_col[:, SM_MF + h:SM_MF + h + 1]
            ic = sm[:, SM_MI + h:SM_MI + h + 1]
            br = b_t[SM_MF + h:SM_MF + h + 1, :]
            ir = s_t[SM_MI + h:SM_MI + h + 1, :]
            m_old = m_ref[h:h + 1, 0:1]
            dmat = jnp.where(causal, bc - br + ir, -jnp.inf)
            inter = bc + m_old
            m_t = jnp.maximum(inter, jnp.max(dmat, axis=1, keepdims=True))
            qh = q_s[rows, h * ML_QK:(h + 1) * ML_QK]
            kh = k_s[rows, h * ML_QK:(h + 1) * ML_QK]
            vh = v_ref[rows, h * ML_V:(h + 1) * ML_V].astype(BF16)
            s = _dot_nt(qh, kh) * jnp.exp(dmat - m_t)
            decay = jnp.exp(inter - m_t)
            ct = ct_ref[h]
            nvec = n_ref[h:h + 1, :]
            num = _dot(s.astype(BF16), vh) + decay * _dot(qh, ct.astype(BF16))
            den = (jnp.sum(s, axis=1, keepdims=True)
                   + decay * jnp.sum(qh.astype(F32) * nvec, axis=1, keepdims=True))
            hout = num / jnp.maximum(jnp.abs(den), jnp.exp(-m_t))
            hn = _rms(hout) * onorm_ref[:, h * ML_V:(h + 1) * ML_V]
            og = _sigmoid(o_ref[rows, h * ML_V:(h + 1) * ML_V])
            y_ref[rows, h * ML_V:(h + 1) * ML_V] = (og * hn).astype(y_ref.dtype)
            b_end = bc[CHUNK - 1:CHUNK, :]
            g_r = b_end - br + ir
            g_c = b_end - bc + ic
            m_new = jnp.maximum(b_end + m_old, jnp.max(g_r, axis=1, keepdims=True))
            wk_c = jnp.exp(g_c - m_new)
            sdec = jnp.exp(b_end + m_old - m_new)
            kw = kh.astype(F32) * wk_c
            ct_ref[h] = sdec * ct + _dot_tn(kw.astype(BF16), vh)
            n_ref[h:h + 1, :] = sdec * nvec + jnp.sum(kw, axis=0, keepdims=True)
            m_ref[h:h + 1, :] = jnp.broadcast_to(m_new, (1, LANES))
        return carry

    lax.fori_loop(0, t_len // CHUNK, chunk, 0)


def _mlstm(proj, conv_w, bias_row, out_norm, t_len=256):
    bsz, s_len, _ = proj.shape
    vw = ML_HEADS * ML_V
    qkw2 = 2 * ML_HEADS * ML_QK
    return pl.pallas_call(
        _mlstm_body,
        grid=(bsz, s_len // t_len),
        in_specs=[
            pl.BlockSpec((None, t_len, qkw2), lambda b, t: (b, t, PJ_MLQK // qkw2)),
            pl.BlockSpec((None, t_len, vw), lambda b, t: (b, t, PJ_MLV // vw)),
            pl.BlockSpec((None, t_len, vw), lambda b, t: (b, t, PJ_MLO // vw)),
            pl.BlockSpec((None, t_len, LANES), lambda b, t: (b, t, PJ_SMALL // LANES)),
            pl.BlockSpec((CONV_K, qkw2), lambda b, t: (0, 0)),
            pl.BlockSpec((1, LANES), lambda b, t: (0, 0)),
            pl.BlockSpec((1, vw), lambda b, t: (0, 0)),
        ],
        out_specs=pl.BlockSpec((None, t_len, vw), lambda b, t: (b, t, 0)),
        out_shape=jax.ShapeDtypeStruct((bsz, s_len, vw), BF16),
        scratch_shapes=[
            pltpu.VMEM((SUBLANES, qkw2), F32),
            pltpu.VMEM((t_len + SUBLANES, qkw2), F32),
            pltpu.VMEM((t_len, ML_HEADS * ML_QK), BF16),
            pltpu.VMEM((t_len, ML_HEADS * ML_QK), BF16),
            pltpu.VMEM((ML_HEADS, ML_QK, ML_V), F32),
            pltpu.VMEM((SUBLANES, ML_QK), F32),
            pltpu.VMEM((SUBLANES, LANES), F32),
        ],
        compiler_params=_cparams(("parallel", "arbitrary")),
    )(proj, proj, proj, proj, conv_w, bias_row, out_norm.reshape(1, vw))


def _ssd_body(xbc_ref, z_ref, sm_ref, conv_ref, convb_ref, dtb_ref, alog_ref, dexp_ref, norm_ref,
              expand_ref, y_ref, carry_ref, xpad_ref, act_s, dt_s, ys_s, st_ref):
    first = pl.program_id(1) == 0
    t_len = xbc_ref.shape[0]
    xw = SSM_HEADS * SSM_P
    gw = SSM_HPG * SSM_P

    @pl.when(first)
    def _():
        st_ref[...] = jnp.zeros_like(st_ref)

    act_s[...] = _silu(_causal_conv(xbc_ref, conv_ref, carry_ref, xpad_ref, first) + convb_ref[...])
    dt_s[...] = _softplus(sm_ref[...] + dtb_ref[...])
    a_row = -jnp.exp(alog_ref[...])

    causal = _tri(CHUNK)
    tri01 = causal.astype(BF16)
    expand = expand_ref[...]

    def chunk(c, carry):
        r0 = pl.multiple_of(c * CHUNK, CHUNK)
        rows = pl.ds(r0, CHUNK)
        dt = dt_s[rows, :]
        seg = _dot01_left(tri01, dt * a_row)
        seg_end = seg[CHUNK - 1:CHUNK, :]
        seg_t = seg.T
        dt_t = dt.T
        p_in = _dot01_right(jnp.exp(seg), expand)
        p_end = _dot01_right(jnp.exp(seg_end - seg) * dt, expand)
        dec = _dot01_right(jnp.broadcast_to(jnp.exp(seg_end), (SUBLANES, LANES)), expand)[0:1, :]
        x = act_s[rows, 0:xw]
        xwt = (x * p_end).astype(BF16)
        for g in range(SSM_G):
            bg = act_s[rows, xw + g * SSM_N:xw + (g + 1) * SSM_N].astype(BF16)
            cg = act_s[rows, xw + (SSM_G + g) * SSM_N:xw + (SSM_G + g + 1) * SSM_N].astype(BF16)
            gmat = _dot_nt(cg, bg)
            st = st_ref[g]
            y_in = _dot(cg, st.astype(BF16)) * p_in[:, g * gw:(g + 1) * gw]
            for hh in range(SSM_HPG):
                h = g * SSM_HPG + hh
                col = SM_DT + h
                seg_c = seg[:, col:col + 1]
                seg_r = seg_t[col:col + 1, :]
                dt_r = dt_t[col:col + 1, :]
                lmat = jnp.exp(jnp.where(causal, seg_c - seg_r, -jnp.inf))
                scores = (gmat * lmat * dt_r).astype(BF16)
                xh = x[:, h * SSM_P:(h + 1) * SSM_P]
                yh = _dot(scores, xh.astype(BF16)) + y_in[:, hh * SSM_P:(hh + 1) * SSM_P]
                ys_s[rows, h * SSM_P:(h + 1) * SSM_P] = yh + dexp_ref[:, h * SSM_P:(h + 1) * SSM_P] * xh
            st_ref[g] = dec[:, g * gw:(g + 1) * gw] * st + _dot_tn(bg, xwt[:, g * gw:(g + 1) * gw])
        return carry

    lax.fori_loop(0, t_len // CHUNK, chunk, 0)
    y = ys_s[...] * _silu(z_ref[...])
    y_ref[...] = (_rms(y) * norm_ref[...]).astype(y_ref.dtype)


def _ssd(proj, conv_w, conv_b, dtb_row, alog_row, d_exp, norm, expand, t_len=256):
    bsz, s_len, _ = proj.shape
    xw = SSM_HEADS * SSM_P
    cw = xw + 2 * SSM_G * SSM_N
    return pl.pallas_call(
        _ssd_body,
        grid=(bsz, s_len // t_len),
        in_specs=[
            pl.BlockSpec((None, t_len, cw), lambda b, t: (b, t, PJ_XBC // cw)),
            pl.BlockSpec((None, t_len, xw), lambda b, t: (b, t, PJ_SZ // xw)),
            pl.BlockSpec((None, t_len, LANES), lambda b, t: (b, t, PJ_SMALL // LANES)),
            pl.BlockSpec((CONV_K, cw), lambda b, t: (0, 0)),
            pl.BlockSpec((1, cw), lambda b, t: (0, 0)),
            pl.BlockSpec((1, LANES), lambda b, t: (0, 0)),
            pl.BlockSpec((1, LANES), lambda b, t: (0, 0)),
            pl.BlockSpec((1, xw), lambda b, t: (0, 0)),
            pl.BlockSpec((1, xw), lambda b, t: (0, 0)),
            pl.BlockSpec((LANES, xw), lambda b, t: (0, 0)),
        ],
        out_specs=pl.BlockSpec((None, t_len, xw), lambda b, t: (b, t, 0)),
        out_shape=jax.ShapeDtypeStruct((bsz, s_len, xw), BF16),
        scratch_shapes=[
            pltpu.VMEM((SUBLANES, cw), F32),
            pltpu.VMEM((t_len + SUBLANES, cw), F32),
            pltpu.VMEM((t_len, cw), F32),
            pltpu.VMEM((t_len, LANES), F32),
            pltpu.VMEM((t_len, xw), F32),
            pltpu.VMEM((SSM_G, SSM_N, SSM_HPG * SSM_P), F32),
        ],
        compiler_params=_cparams(("parallel", "arbitrary")),
    )(proj, proj, proj, conv_w, conv_b.reshape(1, cw), dtb_row, alog_row, d_exp, norm.reshape(1, xw), expand)


def _dsa_prep_body(dq_ref, dkv_ref, iq_ref, sm_ref, qn_ref, kn_ref, kvn_ref, wuk_ref, wuv_ref, ikn_ref,
                   qdt_ref, kd_ref, vt_ref, qit_ref, kidx_ref, widx_ref):
    t_len = dq_ref.shape[0]
    dq = dq_ref[...]
    qd = jnp.concatenate(
        [_rms(dq[:, h * DSA_HD:(h + 1) * DSA_HD]) * qn_ref[...] * (DSA_HD ** -0.5) for h in range(DSA_HEADS)],
        axis=1)
    ckv = (_rms(dkv_ref[...]) * kvn_ref[...]).astype(BF16)
    kraw = _dot(ckv, wuk_ref[...])
    kd_ref[...] = jnp.concatenate(
        [_rms(kraw[:, h * DSA_HD:(h + 1) * DSA_HD]) * kn_ref[...] for h in range(DSA_HEADS)],
        axis=1).astype(BF16)
    vt_ref[...] = _t_blocks_rows(_dot(ckv, wuv_ref[...])).astype(BF16)
    sm = sm_ref[...]
    kidx_ref[...] = (_rms(sm[:, SM_IK:SM_IK + IDX_D]) * ikn_ref[...]).astype(BF16)
    iq = iq_ref[...] * (IDX_D ** -0.5)
    wsc = sm * (IDX_HEADS ** -0.5)
    for j in range(t_len // Q_BLOCK):
        rows = slice(j * Q_BLOCK, (j + 1) * Q_BLOCK)
        qdt_ref[j] = _t_blocks(qd[rows]).astype(BF16)
        qit_ref[j] = _t_blocks(iq[rows]).astype(BF16)
        widx_ref[j] = wsc[rows].T[SM_IW:SM_IW + IDX_HEADS, :]


def _t_blocks_rows(x):
    rows = x.shape[0]
    return jnp.concatenate([_t_blocks(x[r:r + LANES]) for r in range(0, rows, LANES)], axis=1)


def _dsa_prep(proj, q_norm, k_norm, kv_norm, w_uk, w_uv, idx_k_norm, t_len=256):
    bsz, s_len, _ = proj.shape
    nq = s_len // Q_BLOCK
    qpb = t_len // Q_BLOCK
    bw = DSA_HEADS * DSA_HD
    iw = IDX_HEADS * IDX_D
    return pl.pallas_call(
        _dsa_prep_body,
        grid=(bsz, s_len // t_len),
        in_specs=[
            pl.BlockSpec((None, t_len, bw), lambda b, t: (b, t, PJ_DQ // bw)),
            pl.BlockSpec((None, t_len, DSA_LAT), lambda b, t: (b, t, PJ_DKV // DSA_LAT)),
            pl.BlockSpec((None, t_len, iw), lambda b, t: (b, t, PJ_IQ // iw)),
            pl.BlockSpec((None, t_len, LANES), lambda b, t: (b, t, PJ_SMALL // LANES)),
            pl.BlockSpec((1, DSA_HD), lambda b, t: (0, 0)),
            pl.BlockSpec((1, DSA_HD), lambda b, t: (0, 0)),
            pl.BlockSpec((1, DSA_LAT), lambda b, t: (0, 0)),
            pl.BlockSpec((DSA_LAT, bw), lambda b, t: (0, 0)),
            pl.BlockSpec((DSA_LAT, bw), lambda b, t: (0, 0)),
            pl.BlockSpec((1, IDX_D), lambda b, t: (0, 0)),
        ],
        out_specs=[
            pl.BlockSpec((None, qpb, bw, Q_BLOCK), lambda b, t: (b, t, 0, 0)),
            pl.BlockSpec((None, t_len, bw), lambda b, t: (b, t, 0)),
            pl.BlockSpec((None, bw, t_len), lambda b, t: (b, 0, t)),
            pl.BlockSpec((None, qpb, iw, Q_BLOCK), lambda b, t: (b, t, 0, 0)),
            pl.BlockSpec((None, t_len, IDX_D), lambda b, t: (b, t, 0)),
            pl.BlockSpec((None, qpb, IDX_HEADS, Q_BLOCK), lambda b, t: (b, t, 0, 0)),
        ],
        out_shape=[
            jax.ShapeDtypeStruct((bsz, nq, bw, Q_BLOCK), BF16),
            jax.ShapeDtypeStruct((bsz, s_len, bw), BF16),
            jax.ShapeDtypeStruct((bsz, bw, s_len), BF16),
            jax.ShapeDtypeStruct((bsz, nq, iw, Q_BLOCK), BF16),
            jax.ShapeDtypeStruct((bsz, s_len, IDX_D), BF16),
            jax.ShapeDtypeStruct((bsz, nq, IDX_HEADS, Q_BLOCK), F32),
        ],
        compiler_params=_cparams(("parallel", "parallel")),
    )(proj, proj, proj, proj, q_norm.reshape(1, DSA_HD), k_norm.reshape(1, DSA_HD),
      kv_norm.reshape(1, DSA_LAT), w_uk, w_uv, idx_k_norm.reshape(1, IDX_D))


def _dsa_body(qtab, ktab, ltab, kidx_ref, qit_ref, widx_ref, qdt_ref, kd_ref, vt_ref, y_ref,
              keys_ref, thr_ref, m_ref, l_ref, acc_ref, *, tk, top, sub):
    step = pl.program_id(1)
    qi = qtab[step]
    kj = ktab[step]
    n_sub = (ltab[step] + 1) * (tk // sub)

    @pl.when(kj == 0)
    def _():
        lane = lax.broadcasted_iota(I32, (1, Q_BLOCK), 1)
        limit = qi * Q_BLOCK + (lane // CHUNK + 1) * CHUNK
        row = lax.broadcasted_iota(I32, (sub, Q_BLOCK), 0)

        def score_tile(t, carry):
            r0 = pl.multiple_of(t * sub, sub)
            kt = kidx_ref[pl.ds(r0, sub), :]
            acc = jnp.zeros((sub, Q_BLOCK), F32)
            for h in range(IDX_HEADS):
                raw = _dot(kt, qit_ref[h * IDX_D:(h + 1) * IDX_D, :])
                acc = acc + jnp.maximum(raw, 0.0) * widx_ref[h:h + 1, :]
            score = jnp.where(row + r0 < limit, acc, -jnp.inf)
            bits = lax.bitcast_convert_type(score, I32)
            keys_ref[pl.ds(r0, sub), :] = jnp.where(bits < 0, bits ^ jnp.int32(0x7FFFFFFF), bits)
            return carry

        lax.fori_loop(0, n_sub, score_tile, 0)

        def bit_step(i, thr):
            cand = thr + lax.shift_left(jnp.int32(1), jnp.int32(31) - i)

            def count_tile(t, cnt):
                r0 = pl.multiple_of(t * sub, sub)
                ge = keys_ref[pl.ds(r0, sub), :] >= cand
                ones = jnp.where(ge, jnp.int32(1), jnp.int32(0))
                return cnt + jnp.sum(ones.reshape(sub // SUBLANES, SUBLANES, Q_BLOCK), axis=0)

            cnt = lax.fori_loop(0, n_sub, count_tile, jnp.zeros((SUBLANES, Q_BLOCK), I32))
            total = jnp.sum(cnt, axis=0, keepdims=True)
            return jnp.where(total >= top, cand, thr)

        thr = lax.fori_loop(0, 32, bit_step, jnp.full((1, Q_BLOCK), INT_MIN, I32))
        thr = jnp.maximum(thr, jnp.int32(NEG_INF_KEY + 1))
        thr_ref[...] = jnp.broadcast_to(thr, thr_ref.shape)
        m_ref[...] = jnp.full_like(m_ref, NEG_BIG)
        l_ref[...] = jnp.zeros_like(l_ref)
        acc_ref[...] = jnp.zeros_like(acc_ref)

    k0 = pl.multiple_of(kj * tk, tk)
    sel = keys_ref[pl.ds(k0, tk), :] >= thr_ref[0:1, :]
    for h in range(DSA_HEADS):
        hs = slice(h * DSA_HD, (h + 1) * DSA_HD)
        logit = jnp.where(sel, _dot(kd_ref[:, hs], qdt_ref[hs, :]), NEG_BIG)
        m_old = m_ref[h:h + 1, :]
        m_new = jnp.maximum(m_old, jnp.max(logit, axis=0, keepdims=True))
        alpha = jnp.exp(m_old - m_new)
        p = jnp.exp(logit - m_new)
        l_ref[h:h + 1, :] = alpha * l_ref[h:h + 1, :] + jnp.sum(p, axis=0, keepdims=True)
        acc_ref[h] = alpha * acc_ref[h] + _dot(vt_ref[hs, :], p.astype(BF16))
        m_ref[h:h + 1, :] = m_new

    @pl.when(kj == ltab[step])
    def _():
        for h in range(DSA_HEADS):
            o_t = acc_ref[h] / l_ref[h:h + 1, :]
            y_ref[:, h * DSA_HD:(h + 1) * DSA_HD] = o_t.T.astype(y_ref.dtype)


def _dsa_tables(s_len, tk):
    qs, ks, ls = [], [], []
    for qi in range(s_len // Q_BLOCK):
        last = (qi * Q_BLOCK + Q_BLOCK - 1) // tk
        for kj in range(last + 1):
            qs.append(qi)
            ks.append(kj)
            ls.append(last)
    return (np.asarray(qs, np.int32), np.asarray(ks, np.int32), np.asarray(ls, np.int32))


def _dsa(qdt, kd, vt, qit, kidx, widx, tk=512, sub=256):
    bsz, s_len, bw = kd.shape
    iw = IDX_HEADS * IDX_D
    tk = min(tk, s_len)
    top = min(INDEX_TOPK, s_len // 4)
    qtab, ktab, ltab = _dsa_tables(s_len, tk)
    grid_spec = pltpu.PrefetchScalarGridSpec(
        num_scalar_prefetch=3,
        grid=(bsz, len(qtab)),
        in_specs=[
            pl.BlockSpec((None, s_len, IDX_D), lambda b, s, qt, kt, lt: (b, 0, 0)),
            pl.BlockSpec((None, None, iw, Q_BLOCK), lambda b, s, qt, kt, lt: (b, qt[s], 0, 0)),
            pl.BlockSpec((None, None, IDX_HEADS, Q_BLOCK), lambda b, s, qt, kt, lt: (b, qt[s], 0, 0)),
            pl.BlockSpec((None, None, bw, Q_BLOCK), lambda b, s, qt, kt, lt: (b, qt[s], 0, 0)),
            pl.BlockSpec((None, tk, bw), lambda b, s, qt, kt, lt: (b, kt[s], 0)),
            pl.BlockSpec((None, bw, tk), lambda b, s, qt, kt, lt: (b, 0, kt[s])),
        ],
        out_specs=pl.BlockSpec((None, Q_BLOCK, bw), lambda b, s, qt, kt, lt: (b, qt[s], 0)),
        scratch_shapes=[
            pltpu.VMEM((s_len, Q_BLOCK), I32),
            pltpu.VMEM((SUBLANES, Q_BLOCK), I32),
            pltpu.VMEM((DSA_HEADS, Q_BLOCK), F32),
            pltpu.VMEM((DSA_HEADS, Q_BLOCK), F32),
            pltpu.VMEM((DSA_HEADS, DSA_HD, Q_BLOCK), F32),
        ],
    )
    return pl.pallas_call(
        functools.partial(_dsa_body, tk=tk, top=top, sub=min(sub, tk)),
        grid_spec=grid_spec,
        out_shape=jax.ShapeDtypeStruct((bsz, s_len, bw), BF16),
        compiler_params=_cparams(("parallel", "arbitrary")),
    )(jnp.asarray(qtab), jnp.asarray(ktab), jnp.asarray(ltab), kidx, qit, widx, qdt, kd, vt)


def _merge_body(x_ref, g0_ref, g1_ref, g2_ref, ya_ref, yb_ref, yc_ref, wbr_ref, wout_ref, o_ref):
    @pl.when(pl.program_id(1) == 0)
    def _():
        o_ref[...] = x_ref[...]

    merged = (_sigmoid(g0_ref[...]) * _dot(ya_ref[...], wbr_ref[0])
              + _sigmoid(g1_ref[...]) * _dot(yb_ref[...], wbr_ref[1])
              + _sigmoid(g2_ref[...]) * _dot(yc_ref[...], wbr_ref[2]))
    o_ref[...] += _dot(merged.astype(BF16), wout_ref[...])


def _merge(x2, proj2, ya, yb, yc, w_branch, w_out, tm=512, tn=512):
    m, d = x2.shape
    bw = ya.shape[1]
    nj = d // tn
    gate_spec = lambda b: pl.BlockSpec((tm, tn), lambda i, j: (i, PJ_GATES // tn + b * nj + j))
    y_spec = pl.BlockSpec((tm, bw), lambda i, j: (i, 0))
    return pl.pallas_call(
        _merge_body,
        grid=(m // tm, nj),
        in_specs=[
            pl.BlockSpec((tm, d), lambda i, j: (i, 0)),
            gate_spec(0), gate_spec(1), gate_spec(2),
            y_spec, y_spec, y_spec,
            pl.BlockSpec((N_BRANCH, bw, tn), lambda i, j: (0, 0, j)),
            pl.BlockSpec((tn, d), lambda i, j: (j, 0)),
        ],
        out_specs=pl.BlockSpec((tm, d), lambda i, j: (i, 0)),
        out_shape=jax.ShapeDtypeStruct((m, d), F32),
        compiler_params=_cparams(("parallel", "arbitrary")),
    )(x2, proj2, proj2, proj2, ya, yb, yc, w_branch, w_out)


def _xattn_body(x_ref, g_ref, wq_ref, kv_ref, qn_ref, kn_ref, wo_ref, o_ref):
    x = x_ref[...]
    xn = (_rms(x) * g_ref[...]).astype(BF16)
    q = _dot(xn, wq_ref[...])
    kv = kv_ref[...]
    xw = XA_HEADS * XA_HD
    outs = []
    for h in range(XA_HEADS):
        hs = slice(h * XA_HD, (h + 1) * XA_HD)
        qh = (_rms(q[:, hs]) * qn_ref[...] * (XA_HD ** -0.5)).astype(BF16)
        kh = (_rms(kv[:, hs]) * kn_ref[...]).astype(BF16)
        vh = kv[:, xw + h * XA_HD:xw + (h + 1) * XA_HD].astype(BF16)
        s = _dot_nt(qh, kh)
        p = jnp.exp(s - jnp.max(s, axis=1, keepdims=True))
        p = p / jnp.sum(p, axis=1, keepdims=True)
        outs.append(_dot(p.astype(BF16), vh))
    o = jnp.concatenate(outs, axis=1).astype(BF16)
    o_ref[...] = x + _dot(o, wo_ref[...])


def _xattn(x, g, wq, kv, q_norm, k_norm, wo, tm=512):
    bsz, s_len, d = x.shape
    n_mem = kv.shape[1]
    xw = XA_HEADS * XA_HD
    return pl.pallas_call(
        _xattn_body,
        grid=(bsz, s_len // tm),
        in_specs=[
            pl.BlockSpec((None, tm, d), lambda b, i: (b, i, 0)),
            pl.BlockSpec((1, d), lambda b, i: (0, 0)),
            pl.BlockSpec((d, xw), lambda b, i: (0, 0)),
            pl.BlockSpec((None, n_mem, 2 * xw), lambda b, i: (b, 0, 0)),
            pl.BlockSpec((1, XA_HD), lambda b, i: (0, 0)),
            pl.BlockSpec((1, XA_HD), lambda b, i: (0, 0)),
            pl.BlockSpec((xw, d), lambda b, i: (0, 0)),
        ],
        out_specs=pl.BlockSpec((None, tm, d), lambda b, i: (b, i, 0)),
        out_shape=jax.ShapeDtypeStruct((bsz, s_len, d), F32),
        compiler_params=_cparams(("parallel", "parallel")),
    )(x, g.reshape(1, d), wq, kv, q_norm.reshape(1, XA_HD), k_norm.reshape(1, XA_HD), wo)


def _small_row(entries):
    row = jnp.zeros((1, LANES), F32)
    for off, vec in entries:
        row = row.at[0, off:off + vec.shape[0]].set(vec.astype(F32))
    return row


def _mixer(x2, bsz, mix_norm, w_in, ml_conv, ml_i_bias, ml_f_bias, ml_out_norm,
           dsa_q_norm, dsa_k_norm, dsa_kv_norm, dsa_w_uk, dsa_w_uv, idx_k_norm,
           ssm_conv, ssm_conv_b, ssm_dt_bias, ssm_a_log, ssm_d, ssm_norm, w_branch, w_out):
    m, d = x2.shape
    s_len = m // bsz
    proj2 = _proj(x2, mix_norm, _reorder_w_in(w_in))
    proj = proj2.reshape(bsz, s_len, PJ_WIDTH)

    ya = _mlstm(proj, ml_conv, _small_row([(SM_MI, ml_i_bias), (SM_MF, ml_f_bias)]), ml_out_norm)

    head_of_lane = np.arange(SSM_HEADS * SSM_P) // SSM_P
    expand = jnp.asarray(np.arange(LANES)[:, None] == (SM_DT + head_of_lane)[None, :], BF16)
    yc = _ssd(proj, ssm_conv, ssm_conv_b, _small_row([(SM_DT, ssm_dt_bias)]),
              _small_row([(SM_DT, ssm_a_log)]), jnp.repeat(ssm_d, SSM_P).reshape(1, -1), ssm_norm, expand)

    qdt, kd, vt, qit, kidx, widx = _dsa_prep(proj, dsa_q_norm, dsa_k_norm, dsa_kv_norm,
                                             dsa_w_uk.astype(BF16), dsa_w_uv.astype(BF16), idx_k_norm)
    yb = _dsa(qdt, kd, vt, qit, kidx, widx)

    bw = ya.shape[-1]
    return _merge(x2, proj2, ya.reshape(m, bw), yb.reshape(m, bw), yc.reshape(m, bw),
                  w_branch.astype(BF16), w_out.astype(BF16))


def kernel(x, mem, ffn1_norm, ffn1_w_up, ffn1_w_down, mix_norm, w_in, ml_conv, ml_i_bias, ml_f_bias, ml_out_norm, dsa_q_norm, dsa_k_norm, dsa_kv_norm, dsa_w_uk, dsa_w_uv, idx_k_norm, ssm_conv, ssm_conv_b, ssm_dt_bias, ssm_a_log, ssm_d, ssm_norm, w_branch, w_out, xa_norm, xa_mem_norm, xa_wq, xa_wkv, xa_q_norm, xa_k_norm, xa_wo, ffn2_norm, ffn2_w_up, ffn2_w_down):
    bsz, s_len, d = x.shape
    n_mem = mem.shape[1]
    depth = w_in.shape[0]
    x2 = x.reshape(bsz * s_len, d)
    mem2 = mem.reshape(bsz * n_mem, d)
    for l in range(depth):
        x2 = _ffn(x2, ffn1_norm[l], ffn1_w_up[l].astype(BF16), ffn1_w_down[l].astype(BF16))
        x2 = _mixer(x2, bsz, mix_norm[l], w_in[l], ml_conv[l], ml_i_bias[l], ml_f_bias[l], ml_out_norm[l],
                    dsa_q_norm[l], dsa_k_norm[l], dsa_kv_norm[l], dsa_w_uk[l], dsa_w_uv[l], idx_k_norm[l],
                    ssm_conv[l], ssm_conv_b[l], ssm_dt_bias[l], ssm_a_log[l], ssm_d[l], ssm_norm[l],
                    w_branch[l], w_out[l])
        kv = _proj(mem2, xa_mem_norm[l], xa_wkv[l].astype(BF16)).reshape(bsz, n_mem, -1)
        x3 = _xattn(x2.reshape(bsz, s_len, d), xa_norm[l], xa_wq[l].astype(BF16), kv,
                    xa_q_norm[l], xa_k_norm[l], xa_wo[l].astype(BF16))
        x2 = _ffn(x3.reshape(bsz * s_len, d), ffn2_norm[l], ffn2_w_up[l].astype(BF16), ffn2_w_down[l].astype(BF16))
    return x2.reshape(bsz, s_len, d)
```

```python
import functools

import numpy as np
import jax
import jax.numpy as jnp
from jax import lax
from jax.experimental import pallas as pl
from jax.experimental.pallas import tpu as pltpu

F32 = jnp.float32
BF16 = jnp.bfloat16
I32 = jnp.int32

EPS = 1e-6
CHUNK = 64
CONV_K = 4
LANES = 128
SUBLANES = 8
VMEM_LIMIT_BYTES = 56 * 1024 * 1024

ML_HEADS, ML_QK, ML_V = 4, 128, 256
DSA_HEADS, DSA_HD, DSA_LAT = 8, 128, 512
IDX_HEADS, IDX_D, INDEX_TOPK = 16, 64, 256
Q_BLOCK = 128
N_GROUPS = 256
V_ROWS = DSA_HD + 16
SSM_HEADS, SSM_P, SSM_G, SSM_N = 16, 64, 2, 128
SSM_HPG = SSM_HEADS // SSM_G
XA_HEADS, XA_HD = 4, 128
N_BRANCH = 3

SM_IK, SM_IW, SM_DT, SM_MI, SM_MF = 0, 64, 80, 96, 100

NEG_BIG = -1e30
LOG2_E = 1.4426950408889634
INT_MIN = -2 ** 31
NEG_INF_KEY = -2139095041


def _cparams(sem):
    return pltpu.CompilerParams(dimension_semantics=sem, vmem_limit_bytes=VMEM_LIMIT_BYTES)


def _rms(x):
    return x * lax.rsqrt(jnp.mean(x * x, axis=-1, keepdims=True) + EPS)


def _sigmoid(x):
    return 1.0 / (1.0 + jnp.exp(-x))


def _silu(x):
    return x * _sigmoid(x)


def _softplus(x):
    return jnp.maximum(x, 0.0) + jnp.log(1.0 + jnp.exp(-jnp.abs(x)))


def _dot(a, b):
    return jnp.dot(a, b, preferred_element_type=F32)


def _dot_nt(a, b):
    return lax.dot_general(a, b, (((1,), (1,)), ((), ())), preferred_element_type=F32)


def _dot_tn(a, b):
    return lax.dot_general(a, b, (((0,), (0,)), ((), ())), preferred_element_type=F32)


def _split3(x):
    hi = x.astype(BF16)
    r = x - hi.astype(F32)
    mid = r.astype(BF16)
    lo = (r - mid.astype(F32)).astype(BF16)
    return hi, mid, lo


def _dot01_left(m01, x):
    hi, mid, lo = _split3(x)
    return _dot(m01, hi) + _dot(m01, mid) + _dot(m01, lo)


def _dot01_right(x, m01):
    hi, mid, lo = _split3(x)
    return _dot(hi, m01) + _dot(mid, m01) + _dot(lo, m01)


def _tri(n):
    r = lax.broadcasted_iota(I32, (n, n), 0)
    c = lax.broadcasted_iota(I32, (n, n), 1)
    return r >= c


def _t_blocks(x):
    cols = x.shape[1]
    return jnp.concatenate([x[:, c:c + LANES].T for c in range(0, cols, LANES)], axis=0)


def _ffn_body(x_ref, g_ref, wa_ref, wb_ref, wd_ref, o_ref, xn_ref):
    @pl.when(pl.program_id(1) == 0)
    def _():
        x = x_ref[...]
        xn_ref[...] = (_rms(x) * g_ref[...]).astype(BF16)
        o_ref[...] = x

    xn = xn_ref[...]
    a = _dot(xn, wa_ref[...])
    b = _dot(xn, wb_ref[...])
    act = (0.5 * _silu(a) * b).astype(BF16)
    o_ref[...] += _dot(act, wd_ref[...])


def _ffn(x2, g, w_up, w_down, tm=1024, tf=512):
    m, d = x2.shape
    dff = w_down.shape[0]
    nf = dff // tf
    return pl.pallas_call(
        _ffn_body,
        grid=(m // tm, nf),
        in_specs=[
            pl.BlockSpec((tm, d), lambda i, f: (i, 0)),
            pl.BlockSpec((1, d), lambda i, f: (0, 0)),
            pl.BlockSpec((d, tf), lambda i, f: (0, f)),
            pl.BlockSpec((d, tf), lambda i, f: (0, nf + f)),
            pl.BlockSpec((tf, d), lambda i, f: (f, 0)),
        ],
        out_specs=pl.BlockSpec((tm, d), lambda i, f: (i, 0)),
        out_shape=jax.ShapeDtypeStruct((m, d), F32),
        scratch_shapes=[pltpu.VMEM((tm, d), BF16)],
        compiler_params=_cparams(("parallel", "arbitrary")),
    )(x2, g.reshape(1, d), w_up, w_up, w_down)


def _proj_body(x_ref, g_ref, w_ref, o_ref, xn_ref):
    @pl.when(pl.program_id(1) == 0)
    def _():
        xn_ref[...] = (_rms(x_ref[...]) * g_ref[...]).astype(BF16)

    o_ref[...] = _dot(xn_ref[...], w_ref[...])


def _proj(x2, g, w, tm=1024, tn=512):
    m, d = x2.shape
    n = w.shape[1]
    tm = min(tm, m)
    return pl.pallas_call(
        _proj_body,
        grid=(m // tm, n // tn),
        in_specs=[
            pl.BlockSpec((tm, d), lambda i, j: (i, 0)),
            pl.BlockSpec((1, d), lambda i, j: (0, 0)),
            pl.BlockSpec((d, tn), lambda i, j: (0, j)),
        ],
        out_specs=pl.BlockSpec((tm, tn), lambda i, j: (i, j)),
        out_shape=jax.ShapeDtypeStruct((m, n), F32),
        scratch_shapes=[pltpu.VMEM((tm, d), BF16)],
        compiler_params=_cparams(("parallel", "arbitrary")),
    )(x2, g.reshape(1, d), w)


PJ_GATES = 0
PJ_MLQK = 6144
PJ_MLV = 7168
PJ_MLO = 8192
PJ_DQ = 9216
PJ_IQ = 10240
PJ_SZ = 11264
PJ_XBC = 12288
PJ_DKV = 13824
PJ_SMALL = 14336
PJ_WIDTH = 14848


def _reorder_w_in(w_in):
    d = w_in.shape[0]
    sizes = (512, 512, 1024, 1024, 4, 4, 1024, 512, 1024, 64, 16, 1024, 1536, 16, 6144)
    offs = np.concatenate([[0], np.cumsum(sizes)])
    (ml_q, ml_k, ml_v, ml_o, ml_i, ml_f, d_q, d_kv, i_q, i_k, i_w, s_z, s_xbc, s_dt, gates) = [
        w_in[:, int(offs[j]):int(offs[j + 1])] for j in range(len(sizes))]
    small = jnp.concatenate(
        [i_k, i_w, s_dt, ml_i, ml_f, jnp.zeros((d, LANES - 104), w_in.dtype)], axis=1)
    pad = jnp.zeros((d, PJ_WIDTH - PJ_SMALL - LANES), w_in.dtype)
    return jnp.concatenate(
        [gates, ml_q, ml_k, ml_v, ml_o, d_q, i_q, s_z, s_xbc, d_kv, small, pad], axis=1).astype(BF16)


def _causal_conv(x_ref, w_ref, carry_ref, xpad_ref, first):
    t_len = x_ref.shape[0]

    @pl.when(first)
    def _():
        carry_ref[...] = jnp.zeros_like(carry_ref)

    xpad_ref[0:SUBLANES, :] = carry_ref[...]
    xpad_ref[SUBLANES:SUBLANES + t_len, :] = x_ref[...]
    carry_ref[...] = x_ref[t_len - SUBLANES:t_len, :]
    acc = None
    for j in range(CONV_K):
        start = SUBLANES - (CONV_K - 1) + j
        term = w_ref[j:j + 1, :] * xpad_ref[start:start + t_len, :]
        acc = term if acc is None else acc + term
    return acc


def _mlstm_body(qk_ref, v_ref, o_ref, sm_ref, conv_ref, bias_ref, onorm_ref, y_ref,
                carry_ref, xpad_ref, q_s, k_s, ct_ref, n_ref, m_ref):
    first = pl.program_id(1) == 0
    t_len = qk_ref.shape[0]
    qkw = ML_HEADS * ML_QK

    @pl.when(first)
    def _():
        ct_ref[...] = jnp.zeros_like(ct_ref)
        n_ref[...] = jnp.zeros_like(n_ref)
        m_ref[...] = jnp.zeros_like(m_ref)

    qk = _silu(_causal_conv(qk_ref, conv_ref, carry_ref, xpad_ref, first))
    q_s[...] = (qk[:, :qkw] * (ML_QK ** -0.5)).astype(BF16)
    k_s[...] = qk[:, qkw:].astype(BF16)

    causal = _tri(CHUNK)
    tri01 = causal.astype(BF16)

    def chunk(c, carry):
        rows = slice(c * CHUNK, (c + 1) * CHUNK)
        sm = sm_ref[rows, :] + bias_ref[...]
        lf = -_softplus(-sm)
        b_col = _dot01_left(tri01, lf)
        b_t = b_col.T
        s_t = sm.T
        for h in range(ML_HEADS):
            bc = b_col[:, SM_MF + h:SM_MF + h + 1]
            ic = sm[:, SM_MI + h:SM_MI + h + 1]
            br = b_t[SM_MF + h:SM_MF + h + 1, :]
            ir = s_t[SM_MI + h:SM_MI + h + 1, :]
            m_old = m_ref[h:h + 1, 0:1]
            dmat = jnp.where(causal, bc - br + ir, -jnp.inf)
            inter = bc + m_old
            m_t = jnp.maximum(inter, jnp.max(dmat, axis=1, keepdims=True))
            qh = q_s[rows, h * ML_QK:(h + 1) * ML_QK]
            kh = k_s[rows, h * ML_QK:(h + 1) * ML_QK]
            vh = v_ref[rows, h * ML_V:(h + 1) * ML_V].astype(BF16)
            s = _dot_nt(qh, kh) * jnp.exp(dmat - m_t)
            decay = jnp.exp(inter - m_t)
            ct = ct_ref[h]
            nvec = n_ref[h:h + 1, :]
            num = _dot(s.astype(BF16), vh) + decay * _dot(qh, ct.astype(BF16))
            den = (jnp.sum(s, axis=1, keepdims=True)
                   + decay * jnp.sum(qh.astype(F32) * nvec, axis=1, keepdims=True))
            hout = num / jnp.maximum(jnp.abs(den), jnp.exp(-m_t))
            hn = _rms(hout) * onorm_ref[:, h * ML_V:(h + 1) * ML_V]
            og = _sigmoid(o_ref[rows, h * ML_V:(h + 1) * ML_V])
            y_ref[rows, h * ML_V:(h + 1) * ML_V] = (og * hn).astype(y_ref.dtype)
            b_end = bc[CHUNK - 1:CHUNK, :]
            g_r = b_end - br + ir
            g_c = b_end - bc + ic
            m_new = jnp.maximum(b_end + m_old, jnp.max(g_r, axis=1, keepdims=True))
            wk_c = jnp.exp(g_c - m_new)
            sdec = jnp.exp(b_end + m_old - m_new)
            kw = kh.astype(F32) * wk_c
            ct_ref[h] = sdec * ct + _dot_tn(kw.astype(BF16), vh)
            n_ref[h:h + 1, :] = sdec * nvec + jnp.sum(kw, axis=0, keepdims=True)
            m_ref[h:h + 1, :] = jnp.broadcast_to(m_new, (1, LANES))
        return carry

    for c in range(t_len // CHUNK):
        chunk(c, 0)


def _mlstm(proj, conv_w, bias_row, out_norm, t_len=256):
    bsz, s_len, _ = proj.shape
    vw = ML_HEADS * ML_V
    qkw2 = 2 * ML_HEADS * ML_QK
    return pl.pallas_call(
        _mlstm_body,
        grid=(bsz, s_len // t_len),
        in_specs=[
            pl.BlockSpec((None, t_len, qkw2), lambda b, t: (b, t, PJ_MLQK // qkw2)),
            pl.BlockSpec((None, t_len, vw), lambda b, t: (b, t, PJ_MLV // vw)),
            pl.BlockSpec((None, t_len, vw), lambda b, t: (b, t, PJ_MLO // vw)),
            pl.BlockSpec((None, t_len, LANES), lambda b, t: (b, t, PJ_SMALL // LANES)),
            pl.BlockSpec((CONV_K, qkw2), lambda b, t: (0, 0)),
            pl.BlockSpec((1, LANES), lambda b, t: (0, 0)),
            pl.BlockSpec((1, vw), lambda b, t: (0, 0)),
        ],
        out_specs=pl.BlockSpec((None, t_len, vw), lambda b, t: (b, t, 0)),
        out_shape=jax.ShapeDtypeStruct((bsz, s_len, vw), BF16),
        scratch_shapes=[
            pltpu.VMEM((SUBLANES, qkw2), F32),
            pltpu.VMEM((t_len + SUBLANES, qkw2), F32),
            pltpu.VMEM((t_len, ML_HEADS * ML_QK), BF16),
            pltpu.VMEM((t_len, ML_HEADS * ML_QK), BF16),
            pltpu.VMEM((ML_HEADS, ML_QK, ML_V), F32),
            pltpu.VMEM((SUBLANES, ML_QK), F32),
            pltpu.VMEM((SUBLANES, LANES), F32),
        ],
        compiler_params=_cparams(("parallel", "arbitrary")),
    )(proj, proj, proj, proj, conv_w, bias_row, out_norm.reshape(1, vw))


def _ssd_body(xbc_ref, z_ref, sm_ref, conv_ref, convb_ref, dtb_ref, alog_ref, dexp_ref, norm_ref,
              expand_ref, y_ref, carry_ref, xpad_ref, act_s, dt_s, ys_s, st_ref):
    first = pl.program_id(1) == 0
    t_len = xbc_ref.shape[0]
    xw = SSM_HEADS * SSM_P
    gw = SSM_HPG * SSM_P

    @pl.when(first)
    def _():
        st_ref[...] = jnp.zeros_like(st_ref)

    act_s[...] = _silu(_causal_conv(xbc_ref, conv_ref, carry_ref, xpad_ref, first) + convb_ref[...])
    dt_s[...] = _softplus(sm_ref[...] + dtb_ref[...])
    a_row = -jnp.exp(alog_ref[...])

    causal = _tri(CHUNK)
    tri01 = causal.astype(BF16)
    expand = expand_ref[...]

    def chunk(c, carry):
        rows = slice(c * CHUNK, (c + 1) * CHUNK)
        dt = dt_s[rows, :]
        seg = _dot01_left(tri01, dt * a_row)
        seg_end = seg[CHUNK - 1:CHUNK, :]
        seg_t = seg.T
        dt_t = dt.T
        p_in = _dot01_right(jnp.exp(seg), expand)
        p_end = _dot01_right(jnp.exp(seg_end - seg) * dt, expand)
        dec = _dot01_right(jnp.broadcast_to(jnp.exp(seg_end), (SUBLANES, LANES)), expand)[0:1, :]
        x = act_s[rows, 0:xw]
        xwt = (x * p_end).astype(BF16)
        for g in range(SSM_G):
            bg = act_s[rows, xw + g * SSM_N:xw + (g + 1) * SSM_N].astype(BF16)
            cg = act_s[rows, xw + (SSM_G + g) * SSM_N:xw + (SSM_G + g + 1) * SSM_N].astype(BF16)
            gmat = _dot_nt(cg, bg)
            st = st_ref[g]
            y_in = _dot(cg, st.astype(BF16)) * p_in[:, g * gw:(g + 1) * gw]
            for hh in range(SSM_HPG):
                h = g * SSM_HPG + hh
                col = SM_DT + h
                seg_c = seg[:, col:col + 1]
                seg_r = seg_t[col:col + 1, :]
                dt_r = dt_t[col:col + 1, :]
                lmat = jnp.exp(jnp.where(causal, seg_c - seg_r, -jnp.inf))
                scores = (gmat * lmat * dt_r).astype(BF16)
                xh = x[:, h * SSM_P:(h + 1) * SSM_P]
                yh = _dot(scores, xh.astype(BF16)) + y_in[:, hh * SSM_P:(hh + 1) * SSM_P]
                ys_s[rows, h * SSM_P:(h + 1) * SSM_P] = yh + dexp_ref[:, h * SSM_P:(h + 1) * SSM_P] * xh
            st_ref[g] = dec[:, g * gw:(g + 1) * gw] * st + _dot_tn(bg, xwt[:, g * gw:(g + 1) * gw])
        return carry

    for c in range(t_len // CHUNK):
        chunk(c, 0)
    y = ys_s[...] * _silu(z_ref[...])
    y_ref[...] = (_rms(y) * norm_ref[...]).astype(y_ref.dtype)


def _ssd(proj, conv_w, conv_b, dtb_row, alog_row, d_exp, norm, expand, t_len=256):
    bsz, s_len, _ = proj.shape
    xw = SSM_HEADS * SSM_P
    cw = xw + 2 * SSM_G * SSM_N
    return pl.pallas_call(
        _ssd_body,
        grid=(bsz, s_len // t_len),
        in_specs=[
            pl.BlockSpec((None, t_len, cw), lambda b, t: (b, t, PJ_XBC // cw)),
            pl.BlockSpec((None, t_len, xw), lambda b, t: (b, t, PJ_SZ // xw)),
            pl.BlockSpec((None, t_len, LANES), lambda b, t: (b, t, PJ_SMALL // LANES)),
            pl.BlockSpec((CONV_K, cw), lambda b, t: (0, 0)),
            pl.BlockSpec((1, cw), lambda b, t: (0, 0)),
            pl.BlockSpec((1, LANES), lambda b, t: (0, 0)),
            pl.BlockSpec((1, LANES), lambda b, t: (0, 0)),
            pl.BlockSpec((1, xw), lambda b, t: (0, 0)),
            pl.BlockSpec((1, xw), lambda b, t: (0, 0)),
            pl.BlockSpec((LANES, xw), lambda b, t: (0, 0)),
        ],
        out_specs=pl.BlockSpec((None, t_len, xw), lambda b, t: (b, t, 0)),
        out_shape=jax.ShapeDtypeStruct((bsz, s_len, xw), BF16),
        scratch_shapes=[
            pltpu.VMEM((SUBLANES, cw), F32),
            pltpu.VMEM((t_len + SUBLANES, cw), F32),
            pltpu.VMEM((t_len, cw), F32),
            pltpu.VMEM((t_len, LANES), F32),
            pltpu.VMEM((t_len, xw), F32),
            pltpu.VMEM((SSM_G, SSM_N, SSM_HPG * SSM_P), F32),
        ],
        compiler_params=_cparams(("parallel", "arbitrary")),
    )(proj, proj, proj, conv_w, conv_b.reshape(1, cw), dtb_row, alog_row, d_exp, norm.reshape(1, xw), expand)


def _dsa_prep_body(dq_ref, dkv_ref, iq_ref, sm_ref, qn_ref, kn_ref, kvn_ref, wuk_ref, wuv_ref, ikn_ref,
                   qbd_ref, kd_ref, vt_ref, qit_ref, kidx_ref, widx_ref):
    t_len = dq_ref.shape[0]
    dq = dq_ref[...]
    q_scale = (DSA_HD ** -0.5) * LOG2_E
    qd = jnp.concatenate(
        [_rms(dq[:, h * DSA_HD:(h + 1) * DSA_HD]) * qn_ref[...] * q_scale for h in range(DSA_HEADS)],
        axis=1)
    ckv = (_rms(dkv_ref[...]) * kvn_ref[...]).astype(BF16)
    kraw = _dot(ckv, wuk_ref[...])
    kd_ref[...] = jnp.concatenate(
        [_rms(kraw[:, h * DSA_HD:(h + 1) * DSA_HD]) * kn_ref[...] for h in range(DSA_HEADS)],
        axis=1).astype(BF16)
    v_t = _t_blocks_rows(_dot(ckv, wuv_ref[...])).astype(BF16)
    ones = jnp.ones((V_ROWS - DSA_HD, t_len), BF16)
    vt_ref[...] = jnp.concatenate(
        [blk for h in range(DSA_HEADS) for blk in (v_t[h * DSA_HD:(h + 1) * DSA_HD], ones)], axis=0)
    sm = sm_ref[...]
    kidx_ref[...] = (_rms(sm[:, SM_IK:SM_IK + IDX_D]) * ikn_ref[...]).astype(BF16)
    iq = iq_ref[...] * (IDX_D ** -0.5)
    wsc = sm * (IDX_HEADS ** -0.5)
    zero = jnp.zeros((DSA_HD, Q_BLOCK), BF16)
    for j in range(t_len // Q_BLOCK):
        rows = slice(j * Q_BLOCK, (j + 1) * Q_BLOCK)
        qt = _t_blocks(qd[rows]).astype(BF16)
        for p in range(DSA_HEADS // 2):
            top = jnp.concatenate([qt[2 * p * DSA_HD:(2 * p + 1) * DSA_HD], zero], axis=1)
            bot = jnp.concatenate([zero, qt[(2 * p + 1) * DSA_HD:(2 * p + 2) * DSA_HD]], axis=1)
            qbd_ref[j, p] = jnp.concatenate([top, bot], axis=0)
        it = _t_blocks(iq[rows]).astype(BF16)
        qit_ref[j] = jnp.concatenate([it[h * IDX_D:(h + 1) * IDX_D] for h in range(IDX_HEADS)], axis=1)
        wt = wsc[rows].T
        widx_ref[j] = jnp.concatenate([wt[SM_IW + h:SM_IW + h + 1] for h in range(IDX_HEADS)], axis=1)


def _t_blocks_rows(x):
    rows = x.shape[0]
    return jnp.concatenate([_t_blocks(x[r:r + LANES]) for r in range(0, rows, LANES)], axis=1)


def _dsa_prep(proj, q_norm, k_norm, kv_norm, w_uk, w_uv, idx_k_norm, t_len=256):
    bsz, s_len, _ = proj.shape
    nq = s_len // Q_BLOCK
    qpb = t_len // Q_BLOCK
    bw = DSA_HEADS * DSA_HD
    iw = IDX_HEADS * IDX_D
    npair = DSA_HEADS // 2
    return pl.pallas_call(
        _dsa_prep_body,
        grid=(bsz, s_len // t_len),
        in_specs=[
            pl.BlockSpec((None, t_len, bw), lambda b, t: (b, t, PJ_DQ // bw)),
            pl.BlockSpec((None, t_len, DSA_LAT), lambda b, t: (b, t, PJ_DKV // DSA_LAT)),
            pl.BlockSpec((None, t_len, iw), lambda b, t: (b, t, PJ_IQ // iw)),
            pl.BlockSpec((None, t_len, LANES), lambda b, t: (b, t, PJ_SMALL // LANES)),
            pl.BlockSpec((1, DSA_HD), lambda b, t: (0, 0)),
            pl.BlockSpec((1, DSA_HD), lambda b, t: (0, 0)),
            pl.BlockSpec((1, DSA_LAT), lambda b, t: (0, 0)),
            pl.BlockSpec((DSA_LAT, bw), lambda b, t: (0, 0)),
            pl.BlockSpec((DSA_LAT, bw), lambda b, t: (0, 0)),
            pl.BlockSpec((1, IDX_D), lambda b, t: (0, 0)),
        ],
        out_specs=[
            pl.BlockSpec((None, qpb, npair, 2 * DSA_HD, 2 * Q_BLOCK), lambda b, t: (b, t, 0, 0, 0)),
            pl.BlockSpec((None, t_len, bw), lambda b, t: (b, t, 0)),
            pl.BlockSpec((None, DSA_HEADS * V_ROWS, t_len), lambda b, t: (b, 0, t)),
            pl.BlockSpec((None, qpb, IDX_D, IDX_HEADS * Q_BLOCK), lambda b, t: (b, t, 0, 0)),
            pl.BlockSpec((None, t_len, IDX_D), lambda b, t: (b, t, 0)),
            pl.BlockSpec((None, qpb, 1, IDX_HEADS * Q_BLOCK), lambda b, t: (b, t, 0, 0)),
        ],
        out_shape=[
            jax.ShapeDtypeStruct((bsz, nq, npair, 2 * DSA_HD, 2 * Q_BLOCK), BF16),
            jax.ShapeDtypeStruct((bsz, s_len, bw), BF16),
            jax.ShapeDtypeStruct((bsz, DSA_HEADS * V_ROWS, s_len), BF16),
            jax.ShapeDtypeStruct((bsz, nq, IDX_D, IDX_HEADS * Q_BLOCK), BF16),
            jax.ShapeDtypeStruct((bsz, s_len, IDX_D), BF16),
            jax.ShapeDtypeStruct((bsz, nq, 1, IDX_HEADS * Q_BLOCK), F32),
        ],
        compiler_params=_cparams(("parallel", "parallel")),
    )(proj, proj, proj, proj, q_norm.reshape(1, DSA_HD), k_norm.reshape(1, DSA_HD),
      kv_norm.reshape(1, DSA_LAT), w_uk, w_uv, idx_k_norm.reshape(1, IDX_D))


def _dsa_body(qtab, ktab, ltab, kidx_ref, qit_ref, widx_ref, qbd_ref, kd_ref, vt_ref, y_ref,
              keys_ref, gmax_ref, thr_ref, m_ref, acc_ref, *pair_refs, tk, top, sub_a, sub_b, sub_c):
    lg_refs = pair_refs[:DSA_HEADS // 2]
    pb_refs = pair_refs[DSA_HEADS // 2:]
    step = pl.program_id(1)
    qi = qtab[step]
    kj = ktab[step]
    n_keys_tiles = ltab[step] + 1
    pair_w = 2 * Q_BLOCK

    @pl.when(kj == 0)
    def _():
        lane = lax.broadcasted_iota(I32, (1, Q_BLOCK), 1)
        limit = qi * Q_BLOCK + (lane // CHUNK + 1) * CHUNK
        row = lax.broadcasted_iota(I32, (sub_a, Q_BLOCK), 0)

        def score_tile(t, carry):
            for c in range(tk // sub_a):
                r0 = pl.multiple_of(t * tk + c * sub_a, sub_a)
                kt = kidx_ref[pl.ds(r0, sub_a), :]
                acc2 = jnp.zeros((sub_a, pair_w), F32)
                for p in range(IDX_HEADS // 2):
                    cols = slice(p * pair_w, (p + 1) * pair_w)
                    acc2 = acc2 + jnp.maximum(_dot(kt, qit_ref[:, cols]), 0.0) * widx_ref[:, cols]
                acc = acc2[:, :Q_BLOCK] + acc2[:, Q_BLOCK:]
                score = jnp.where(row + r0 < limit, acc, -jnp.inf)
                bits = lax.bitcast_convert_type(score, I32)
                key = jnp.where(bits < 0, bits ^ jnp.int32(0x7FFFFFFF), bits)
                keys_ref[pl.ds(r0, sub_a), :] = key
                g0 = (c * sub_a) % N_GROUPS
                gmax_ref[g0:g0 + sub_a, :] = jnp.maximum(gmax_ref[g0:g0 + sub_a, :], key)
            return carry

        gmax_ref[...] = jnp.full_like(gmax_ref, INT_MIN)
        lax.fori_loop(0, n_keys_tiles, score_tile, 0)

        gm = gmax_ref[...]
        k_hi = jnp.max(gm, axis=0, keepdims=True)
        k_lo = jnp.min(gm, axis=0, keepdims=True)
        n_bits = jnp.int32(32) - jnp.min(lax.clz(k_hi ^ k_lo))
        low_mask = jnp.where(n_bits >= 32, jnp.int32(-1), lax.shift_left(jnp.int32(1), n_bits) - 1)
        t_init = ((k_hi ^ INT_MIN) & ~low_mask) ^ INT_MIN
        n_acc = 4 * SUBLANES
        n_b_tiles = n_keys_tiles * (tk // sub_b)

        def bit_step(i, thr):
            cand = thr + lax.shift_left(jnp.int32(1), n_bits - 1 - i)

            def count_tile(t, cnt):
                r0 = pl.multiple_of(t * sub_b, sub_b)
                ge = keys_ref[pl.ds(r0, sub_b), :] >= cand
                ones = jnp.where(ge, jnp.int32(1), jnp.int32(0))
                return cnt + jnp.sum(ones.reshape(sub_b // n_acc, n_acc, Q_BLOCK), axis=0)

            cnt = lax.fori_loop(0, n_b_tiles, count_tile, jnp.zeros((n_acc, Q_BLOCK), I32))
            total = jnp.sum(cnt, axis=0, keepdims=True)
            return jnp.where(total >= top, cand, thr)

        thr = lax.fori_loop(0, n_bits, bit_step, t_init)
        thr = jnp.maximum(thr, jnp.int32(NEG_INF_KEY + 1))
        thr_ref[...] = jnp.broadcast_to(thr, thr_ref.shape)
        m_ref[...] = jnp.full_like(m_ref, NEG_BIG)
        acc_ref[...] = jnp.zeros_like(acc_ref)

    k0 = pl.multiple_of(kj * tk, tk)
    npair = DSA_HEADS // 2
    thr = thr_ref[0:1, :]

    def logits(p):
        lg_refs[p][...] = _dot(kd_ref[:, p * 2 * DSA_HD:(p + 1) * 2 * DSA_HD], qbd_ref[p])

    def mask_pair(p):
        out = [jnp.full((SUBLANES, Q_BLOCK), NEG_BIG, F32) for _ in range(2)]
        for c in range(tk // sub_c):
            rows = slice(c * sub_c, (c + 1) * sub_c)
            sel = keys_ref[pl.ds(k0 + c * sub_c, sub_c), :] >= thr
            for j in range(2):
                qs = slice(j * Q_BLOCK, (j + 1) * Q_BLOCK)
                lg = jnp.where(sel, lg_refs[p][rows, qs], NEG_BIG)
                lg_refs[p][rows, qs] = lg
                out[j] = jnp.maximum(out[j], jnp.max(lg.reshape(sub_c // SUBLANES, SUBLANES, Q_BLOCK), axis=0))
        return out

    def values(p, mx):
        m_old = m_ref[p:p + 1, :]
        tile_max = jnp.concatenate([jnp.max(mx[j], axis=0, keepdims=True) for j in range(2)], axis=1)
        m_new = jnp.maximum(m_old, tile_max)
        alpha = jnp.exp2(m_old - m_new)
        m_ref[p:p + 1, :] = m_new
        pb_refs[p][...] = jnp.exp2(lg_refs[p][...] - m_new).astype(BF16)
        for j in range(2):
            h = 2 * p + j
            qs = slice(j * Q_BLOCK, (j + 1) * Q_BLOCK)
            acc_ref[h] = (alpha[:, qs] * acc_ref[h]
                          + _dot(vt_ref[h * V_ROWS:(h + 1) * V_ROWS, :], pb_refs[p][:, qs]))

    logits(0)
    logits(1)
    mx0 = mask_pair(0)
    logits(2)
    mx1 = mask_pair(1)
    values(0, mx0)
    logits(3)
    mx2 = mask_pair(2)
    values(1, mx1)
    mx3 = mask_pair(3)
    values(2, mx2)
    values(3, mx3)

    @pl.when(kj == ltab[step])
    def _():
        for h in range(DSA_HEADS):
            acc = acc_ref[h]
            o_t = acc[:DSA_HD] / acc[DSA_HD:DSA_HD + 1]
            y_ref[:, h * DSA_HD:(h + 1) * DSA_HD] = o_t.T.astype(y_ref.dtype)


def _dsa_tables(s_len, tk):
    qs, ks, ls = [], [], []
    for qi in range(s_len // Q_BLOCK):
        last = (qi * Q_BLOCK + Q_BLOCK - 1) // tk
        for kj in range(last + 1):
            qs.append(qi)
            ks.append(kj)
            ls.append(last)
    return (np.asarray(qs, np.int32), np.asarray(ks, np.int32), np.asarray(ls, np.int32))


def _dsa(qbd, kd, vt, qit, kidx, widx, tk=1024, sub_a=128, sub_b=1024, sub_c=128):
    bsz, s_len, bw = kd.shape
    npair = DSA_HEADS // 2
    tk = min(tk, s_len)
    top = min(INDEX_TOPK, s_len // 4)
    qtab, ktab, ltab = _dsa_tables(s_len, tk)
    grid_spec = pltpu.PrefetchScalarGridSpec(
        num_scalar_prefetch=3,
        grid=(bsz, len(qtab)),
        in_specs=[
            pl.BlockSpec((None, s_len, IDX_D), lambda b, s, qt, kt, lt: (b, 0, 0)),
            pl.BlockSpec((None, None, IDX_D, IDX_HEADS * Q_BLOCK), lambda b, s, qt, kt, lt: (b, qt[s], 0, 0)),
            pl.BlockSpec((None, None, 1, IDX_HEADS * Q_BLOCK), lambda b, s, qt, kt, lt: (b, qt[s], 0, 0)),
            pl.BlockSpec((None, None, npair, 2 * DSA_HD, 2 * Q_BLOCK), lambda b, s, qt, kt, lt: (b, qt[s], 0, 0, 0)),
            pl.BlockSpec((None, tk, bw), lambda b, s, qt, kt, lt: (b, kt[s], 0)),
            pl.BlockSpec((None, DSA_HEADS * V_ROWS, tk), lambda b, s, qt, kt, lt: (b, 0, kt[s])),
        ],
        out_specs=pl.BlockSpec((None, Q_BLOCK, bw), lambda b, s, qt, kt, lt: (b, qt[s], 0)),
        scratch_shapes=[
            pltpu.VMEM((s_len, Q_BLOCK), I32),
            pltpu.VMEM((N_GROUPS, Q_BLOCK), I32),
            pltpu.VMEM((SUBLANES, Q_BLOCK), I32),
            pltpu.VMEM((SUBLANES, 2 * Q_BLOCK), F32),
            pltpu.VMEM((DSA_HEADS, V_ROWS, Q_BLOCK), F32),
        ] + [pltpu.VMEM((tk, 2 * Q_BLOCK), F32) for _ in range(npair)] + [
            pltpu.VMEM((tk, 2 * Q_BLOCK), BF16) for _ in range(npair)
        ],
    )
    return pl.pallas_call(
        functools.partial(_dsa_body, tk=tk, top=top, sub_a=sub_a, sub_b=min(sub_b, tk), sub_c=sub_c),
        grid_spec=grid_spec,
        out_shape=jax.ShapeDtypeStruct((bsz, s_len, bw), BF16),
        compiler_params=_cparams(("parallel", "arbitrary")),
    )(jnp.asarray(qtab), jnp.asarray(ktab), jnp.asarray(ltab), kidx, qit, widx, qbd, kd, vt)


def _merge_body(x_ref, g0_ref, g1_ref, g2_ref, ya_ref, yb_ref, yc_ref, wbr_ref, wout_ref, o_ref):
    @pl.when(pl.program_id(1) == 0)
    def _():
        o_ref[...] = x_ref[...]

    merged = (_sigmoid(g0_ref[...]) * _dot(ya_ref[...], wbr_ref[0])
              + _sigmoid(g1_ref[...]) * _dot(yb_ref[...], wbr_ref[1])
              + _sigmoid(g2_ref[...]) * _dot(yc_ref[...], wbr_ref[2]))
    o_ref[...] += _dot(merged.astype(BF16), wout_ref[...])


def _merge(x2, proj2, ya, yb, yc, w_branch, w_out, tm=512, tn=512):
    m, d = x2.shape
    bw = ya.shape[1]
    nj = d // tn
    gate_spec = lambda b: pl.BlockSpec((tm, tn), lambda i, j: (i, PJ_GATES // tn + b * nj + j))
    y_spec = pl.BlockSpec((tm, bw), lambda i, j: (i, 0))
    return pl.pallas_call(
        _merge_body,
        grid=(m // tm, nj),
        in_specs=[
            pl.BlockSpec((tm, d), lambda i, j: (i, 0)),
            gate_spec(0), gate_spec(1), gate_spec(2),
            y_spec, y_spec, y_spec,
            pl.BlockSpec((N_BRANCH, bw, tn), lambda i, j: (0, 0, j)),
            pl.BlockSpec((tn, d), lambda i, j: (j, 0)),
        ],
        out_specs=pl.BlockSpec((tm, d), lambda i, j: (i, 0)),
        out_shape=jax.ShapeDtypeStruct((m, d), F32),
        compiler_params=_cparams(("parallel", "arbitrary")),
    )(x2, proj2, proj2, proj2, ya, yb, yc, w_branch, w_out)


def _xattn_body(x_ref, g_ref, wq_ref, kv_ref, qn_ref, kn_ref, wo_ref, o_ref):
    x = x_ref[...]
    xn = (_rms(x) * g_ref[...]).astype(BF16)
    q = _dot(xn, wq_ref[...])
    kv = kv_ref[...]
    xw = XA_HEADS * XA_HD
    outs = []
    for h in range(XA_HEADS):
        hs = slice(h * XA_HD, (h + 1) * XA_HD)
        qh = (_rms(q[:, hs]) * qn_ref[...] * (XA_HD ** -0.5)).astype(BF16)
        kh = (_rms(kv[:, hs]) * kn_ref[...]).astype(BF16)
        vh = kv[:, xw + h * XA_HD:xw + (h + 1) * XA_HD].astype(BF16)
        s = _dot_nt(qh, kh)
        p = jnp.exp(s - jnp.max(s, axis=1, keepdims=True))
        p = p / jnp.sum(p, axis=1, keepdims=True)
        outs.append(_dot(p.astype(BF16), vh))
    o = jnp.concatenate(outs, axis=1).astype(BF16)
    o_ref[...] = x + _dot(o, wo_ref[...])


def _xattn(x, g, wq, kv, q_norm, k_norm, wo, tm=512):
    bsz, s_len, d = x.shape
    n_mem = kv.shape[1]
    xw = XA_HEADS * XA_HD
    return pl.pallas_call(
        _xattn_body,
        grid=(bsz, s_len // tm),
        in_specs=[
            pl.BlockSpec((None, tm, d), lambda b, i: (b, i, 0)),
            pl.BlockSpec((1, d), lambda b, i: (0, 0)),
            pl.BlockSpec((d, xw), lambda b, i: (0, 0)),
            pl.BlockSpec((None, n_mem, 2 * xw), lambda b, i: (b, 0, 0)),
            pl.BlockSpec((1, XA_HD), lambda b, i: (0, 0)),
            pl.BlockSpec((1, XA_HD), lambda b, i: (0, 0)),
            pl.BlockSpec((xw, d), lambda b, i: (0, 0)),
        ],
        out_specs=pl.BlockSpec((None, tm, d), lambda b, i: (b, i, 0)),
        out_shape=jax.ShapeDtypeStruct((bsz, s_len, d), F32),
        compiler_params=_cparams(("parallel", "parallel")),
    )(x, g.reshape(1, d), wq, kv, q_norm.reshape(1, XA_HD), k_norm.reshape(1, XA_HD), wo)


def _small_row(entries):
    row = jnp.zeros((1, LANES), F32)
    for off, vec in entries:
        row = row.at[0, off:off + vec.shape[0]].set(vec.astype(F32))
    return row


def _mixer(x2, bsz, mix_norm, w_in, ml_conv, ml_i_bias, ml_f_bias, ml_out_norm,
           dsa_q_norm, dsa_k_norm, dsa_kv_norm, dsa_w_uk, dsa_w_uv, idx_k_norm,
           ssm_conv, ssm_conv_b, ssm_dt_bias, ssm_a_log, ssm_d, ssm_norm, w_branch, w_out):
    m, d = x2.shape
    s_len = m // bsz
    proj2 = _proj(x2, mix_norm, _reorder_w_in(w_in))
    proj = proj2.reshape(bsz, s_len, PJ_WIDTH)

    ya = _mlstm(proj, ml_conv, _small_row([(SM_MI, ml_i_bias), (SM_MF, ml_f_bias)]), ml_out_norm)

    head_of_lane = np.arange(SSM_HEADS * SSM_P) // SSM_P
    expand = jnp.asarray(np.arange(LANES)[:, None] == (SM_DT + head_of_lane)[None, :], BF16)
    yc = _ssd(proj, ssm_conv, ssm_conv_b, _small_row([(SM_DT, ssm_dt_bias)]),
              _small_row([(SM_DT, ssm_a_log)]), jnp.repeat(ssm_d, SSM_P).reshape(1, -1), ssm_norm, expand)

    qbd, kd, vt, qit, kidx, widx = _dsa_prep(proj, dsa_q_norm, dsa_k_norm, dsa_kv_norm,
                                             dsa_w_uk.astype(BF16), dsa_w_uv.astype(BF16), idx_k_norm)
    yb = _dsa(qbd, kd, vt, qit, kidx, widx)

    bw = ya.shape[-1]
    return _merge(x2, proj2, ya.reshape(m, bw), yb.reshape(m, bw), yc.reshape(m, bw),
                  w_branch.astype(BF16), w_out.astype(BF16))


def kernel(x, mem, ffn1_norm, ffn1_w_up, ffn1_w_down, mix_norm, w_in, ml_conv, ml_i_bias, ml_f_bias, ml_out_norm, dsa_q_norm, dsa_k_norm, dsa_kv_norm, dsa_w_uk, dsa_w_uv, idx_k_norm, ssm_conv, ssm_conv_b, ssm_dt_bias, ssm_a_log, ssm_d, ssm_norm, w_branch, w_out, xa_norm, xa_mem_norm, xa_wq, xa_wkv, xa_q_norm, xa_k_norm, xa_wo, ffn2_norm, ffn2_w_up, ffn2_w_down):
    bsz, s_len, d = x.shape
    n_mem = mem.shape[1]
    depth = w_in.shape[0]
    x2 = x.reshape(bsz * s_len, d)
    mem2 = mem.reshape(bsz * n_mem, d)
    for l in range(depth):
        x2 = _ffn(x2, ffn1_norm[l], ffn1_w_up[l].astype(BF16), ffn1_w_down[l].astype(BF16))
        x2 = _mixer(x2, bsz, mix_norm[l], w_in[l], ml_conv[l], ml_i_bias[l], ml_f_bias[l], ml_out_norm[l],
                    dsa_q_norm[l], dsa_k_norm[l], dsa_kv_norm[l], dsa_w_uk[l], dsa_w_uv[l], idx_k_norm[l],
                    ssm_conv[l], ssm_conv_b[l], ssm_dt_bias[l], ssm_a_log[l], ssm_d[l], ssm_norm[l],
                    w_branch[l], w_out[l])
        kv = _proj(mem2, xa_mem_norm[l], xa_wkv[l].astype(BF16)).reshape(bsz, n_mem, -1)
        x3 = _xattn(x2.reshape(bsz, s_len, d), xa_norm[l], xa_wq[l].astype(BF16), kv,
                    xa_q_norm[l], xa_k_norm[l], xa_wo[l].astype(BF16))
        x2 = _ffn(x3.reshape(bsz * s_len, d), ffn2_norm[l], ffn2_w_up[l].astype(BF16), ffn2_w_down[l].astype(BF16))
    return x2.reshape(bsz, s_len, d)
```

```python
import functools

import numpy as np
import jax
import jax.numpy as jnp
from jax import lax
from jax.experimental import pallas as pl
from jax.experimental.pallas import tpu as pltpu

F32 = jnp.float32
BF16 = jnp.bfloat16
I32 = jnp.int32

EPS = 1e-6
CHUNK = 64
CONV_K = 4
LANES = 128
SUBLANES = 8
VMEM_LIMIT_BYTES = 56 * 1024 * 1024

ML_HEADS, ML_QK, ML_V = 4, 128, 256
DSA_HEADS, DSA_HD, DSA_LAT = 8, 128, 512
IDX_HEADS, IDX_D, INDEX_TOPK = 16, 64, 256
Q_BLOCK = 128
N_GROUPS = 256
V_ROWS = DSA_HD + 16
SSM_HEADS, SSM_P, SSM_G, SSM_N = 16, 64, 2, 128
SSM_HPG = SSM_HEADS // SSM_G
XA_HEADS, XA_HD = 4, 128
N_BRANCH = 3

SM_IK, SM_IW, SM_DT, SM_MI, SM_MF = 0, 64, 80, 96, 100

NEG_BIG = -1e30
LOG2_E = 1.4426950408889634
INT_MIN = -2 ** 31
NEG_INF_KEY = -2139095041


def _cparams(sem):
    return pltpu.CompilerParams(dimension_semantics=sem, vmem_limit_bytes=VMEM_LIMIT_BYTES)


def _rms(x):
    return x * lax.rsqrt(jnp.mean(x * x, axis=-1, keepdims=True) + EPS)


def _sigmoid(x):
    return 1.0 / (1.0 + jnp.exp(-x))


def _silu(x):
    return x * _sigmoid(x)


def _softplus(x):
    return jnp.maximum(x, 0.0) + jnp.log(1.0 + jnp.exp(-jnp.abs(x)))


def _dot(a, b):
    return jnp.dot(a, b, preferred_element_type=F32)


def _dot_nt(a, b):
    return lax.dot_general(a, b, (((1,), (1,)), ((), ())), preferred_element_type=F32)


def _dot_tn(a, b):
    return lax.dot_general(a, b, (((0,), (0,)), ((), ())), preferred_element_type=F32)


def _split3(x):
    hi = x.astype(BF16)
    r = x - hi.astype(F32)
    mid = r.astype(BF16)
    lo = (r - mid.astype(F32)).astype(BF16)
    return hi, mid, lo


def _dot01_left(m01, x):
    hi, mid, lo = _split3(x)
    return _dot(m01, hi) + _dot(m01, mid) + _dot(m01, lo)


def _dot01_right(x, m01):
    hi, mid, lo = _split3(x)
    return _dot(hi, m01) + _dot(mid, m01) + _dot(lo, m01)


def _tri(n):
    r = lax.broadcasted_iota(I32, (n, n), 0)
    c = lax.broadcasted_iota(I32, (n, n), 1)
    return r >= c


def _t_blocks(x):
    cols = x.shape[1]
    return jnp.concatenate([x[:, c:c + LANES].T for c in range(0, cols, LANES)], axis=0)


def _ffn_body(x_ref, g_ref, wa_ref, wb_ref, wd_ref, o_ref, xn_ref):
    @pl.when(pl.program_id(1) == 0)
    def _():
        x = x_ref[...]
        xn_ref[...] = (_rms(x) * g_ref[...]).astype(BF16)
        o_ref[...] = x

    xn = xn_ref[...]
    a = _dot(xn, wa_ref[...])
    b = _dot(xn, wb_ref[...])
    act = (0.5 * _silu(a) * b).astype(BF16)
    o_ref[...] += _dot(act, wd_ref[...])


def _ffn(x2, g, w_up, w_down, tm=1024, tf=512):
    m, d = x2.shape
    dff = w_down.shape[0]
    nf = dff // tf
    return pl.pallas_call(
        _ffn_body,
        grid=(m // tm, nf),
        in_specs=[
            pl.BlockSpec((tm, d), lambda i, f: (i, 0)),
            pl.BlockSpec((1, d), lambda i, f: (0, 0)),
            pl.BlockSpec((d, tf), lambda i, f: (0, f)),
            pl.BlockSpec((d, tf), lambda i, f: (0, nf + f)),
            pl.BlockSpec((tf, d), lambda i, f: (f, 0)),
        ],
        out_specs=pl.BlockSpec((tm, d), lambda i, f: (i, 0)),
        out_shape=jax.ShapeDtypeStruct((m, d), F32),
        scratch_shapes=[pltpu.VMEM((tm, d), BF16)],
        compiler_params=_cparams(("parallel", "arbitrary")),
    )(x2, g.reshape(1, d), w_up, w_up, w_down)


def _proj_body(x_ref, g_ref, w_ref, o_ref, xn_ref):
    @pl.when(pl.program_id(1) == 0)
    def _():
        xn_ref[...] = (_rms(x_ref[...]) * g_ref[...]).astype(BF16)

    o_ref[...] = _dot(xn_ref[...], w_ref[...])


def _proj(x2, g, w, tm=1024, tn=512):
    m, d = x2.shape
    n = w.shape[1]
    tm = min(tm, m)
    return pl.pallas_call(
        _proj_body,
        grid=(m // tm, n // tn),
        in_specs=[
            pl.BlockSpec((tm, d), lambda i, j: (i, 0)),
            pl.BlockSpec((1, d), lambda i, j: (0, 0)),
            pl.BlockSpec((d, tn), lambda i, j: (0, j)),
        ],
        out_specs=pl.BlockSpec((tm, tn), lambda i, j: (i, j)),
        out_shape=jax.ShapeDtypeStruct((m, n), F32),
        scratch_shapes=[pltpu.VMEM((tm, d), BF16)],
        compiler_params=_cparams(("parallel", "arbitrary")),
    )(x2, g.reshape(1, d), w)


PJ_GATES = 0
PJ_MLQK = 6144
PJ_MLV = 7168
PJ_MLO = 8192
PJ_DQ = 9216
PJ_IQ = 10240
PJ_SZ = 11264
PJ_XBC = 12288
PJ_DKV = 13824
PJ_SMALL = 14336
PJ_WIDTH = 15360


def _reorder_w_in(w_in):
    d = w_in.shape[0]
    sizes = (512, 512, 1024, 1024, 4, 4, 1024, 512, 1024, 64, 16, 1024, 1536, 16, 6144)
    offs = np.concatenate([[0], np.cumsum(sizes)])
    (ml_q, ml_k, ml_v, ml_o, ml_i, ml_f, d_q, d_kv, i_q, i_k, i_w, s_z, s_xbc, s_dt, gates) = [
        w_in[:, int(offs[j]):int(offs[j + 1])] for j in range(len(sizes))]
    small = jnp.concatenate(
        [i_k, i_w, s_dt, ml_i, ml_f, jnp.zeros((d, LANES - 104), w_in.dtype)], axis=1)
    pad = jnp.zeros((d, PJ_WIDTH - PJ_SMALL - LANES), w_in.dtype)
    return jnp.concatenate(
        [gates, ml_q, ml_k, ml_v, ml_o, d_q, i_q, s_z, s_xbc, d_kv, small, pad], axis=1).astype(BF16)


def _causal_conv(x_ref, w_ref, carry_ref, xpad_ref, first):
    t_len = x_ref.shape[0]

    @pl.when(first)
    def _():
        carry_ref[...] = jnp.zeros_like(carry_ref)

    xpad_ref[0:SUBLANES, :] = carry_ref[...]
    xpad_ref[SUBLANES:SUBLANES + t_len, :] = x_ref[...]
    carry_ref[...] = x_ref[t_len - SUBLANES:t_len, :]
    acc = None
    for j in range(CONV_K):
        start = SUBLANES - (CONV_K - 1) + j
        term = w_ref[j:j + 1, :] * xpad_ref[start:start + t_len, :]
        acc = term if acc is None else acc + term
    return acc


def _mlstm_body(qk_ref, v_ref, o_ref, sm_ref, conv_ref, bias_ref, onorm_ref, y_ref,
                carry_ref, xpad_ref, q_s, k_s, ct_ref, n_ref, m_ref):
    first = pl.program_id(1) == 0
    t_len = qk_ref.shape[0]
    qkw = ML_HEADS * ML_QK

    @pl.when(first)
    def _():
        ct_ref[...] = jnp.zeros_like(ct_ref)
        n_ref[...] = jnp.zeros_like(n_ref)
        m_ref[...] = jnp.zeros_like(m_ref)

    qk = _silu(_causal_conv(qk_ref, conv_ref, carry_ref, xpad_ref, first))
    q_s[...] = (qk[:, :qkw] * (ML_QK ** -0.5)).astype(BF16)
    k_s[...] = qk[:, qkw:].astype(BF16)

    causal = _tri(CHUNK)
    tri01 = causal.astype(BF16)

    def chunk(c, carry):
        rows = slice(c * CHUNK, (c + 1) * CHUNK)
        sm = sm_ref[rows, :] + bias_ref[...]
        lf = -_softplus(-sm)
        b_col = _dot01_left(tri01, lf)
        b_t = b_col.T
        s_t = sm.T
        for h in range(ML_HEADS):
            bc = b_col[:, SM_MF + h:SM_MF + h + 1]
            ic = sm[:, SM_MI + h:SM_MI + h + 1]
            br = b_t[SM_MF + h:SM_MF + h + 1, :]
            ir = s_t[SM_MI + h:SM_MI + h + 1, :]
            m_old = m_ref[h:h + 1, 0:1]
            dmat = jnp.where(causal, bc - br + ir, -jnp.inf)
            inter = bc + m_old
            m_t = jnp.maximum(inter, jnp.max(dmat, axis=1, keepdims=True))
            qh = q_s[rows, h * ML_QK:(h + 1) * ML_QK]
            kh = k_s[rows, h * ML_QK:(h + 1) * ML_QK]
            vh = v_ref[rows, h * ML_V:(h + 1) * ML_V].astype(BF16)
            s = _dot_nt(qh, kh) * jnp.exp(dmat - m_t)
            decay = jnp.exp(inter - m_t)
            ct = ct_ref[h]
            nvec = n_ref[h:h + 1, :]
            num = _dot(s.astype(BF16), vh) + decay * _dot(qh, ct.astype(BF16))
            den = (jnp.sum(s, axis=1, keepdims=True)
                   + decay * jnp.sum(qh.astype(F32) * nvec, axis=1, keepdims=True))
            hout = num / jnp.maximum(jnp.abs(den), jnp.exp(-m_t))
            hn = _rms(hout) * onorm_ref[:, h * ML_V:(h + 1) * ML_V]
            og = _sigmoid(o_ref[rows, h * ML_V:(h + 1) * ML_V])
            y_ref[rows, h * ML_V:(h + 1) * ML_V] = (og * hn).astype(y_ref.dtype)
            b_end = bc[CHUNK - 1:CHUNK, :]
            g_r = b_end - br + ir
            g_c = b_end - bc + ic
            m_new = jnp.maximum(b_end + m_old, jnp.max(g_r, axis=1, keepdims=True))
            wk_c = jnp.exp(g_c - m_new)
            sdec = jnp.exp(b_end + m_old - m_new)
            kw = kh.astype(F32) * wk_c
            ct_ref[h] = sdec * ct + _dot_tn(kw.astype(BF16), vh)
            n_ref[h:h + 1, :] = sdec * nvec + jnp.sum(kw, axis=0, keepdims=True)
            m_ref[h:h + 1, :] = jnp.broadcast_to(m_new, (1, LANES))
        return carry

    for c in range(t_len // CHUNK):
        chunk(c, 0)


def _mlstm(proj, conv_w, bias_row, out_norm, t_len=256):
    bsz, s_len, _ = proj.shape
    vw = ML_HEADS * ML_V
    qkw2 = 2 * ML_HEADS * ML_QK
    return pl.pallas_call(
        _mlstm_body,
        grid=(bsz, s_len // t_len),
        in_specs=[
            pl.BlockSpec((None, t_len, qkw2), lambda b, t: (b, t, PJ_MLQK // qkw2)),
            pl.BlockSpec((None, t_len, vw), lambda b, t: (b, t, PJ_MLV // vw)),
            pl.BlockSpec((None, t_len, vw), lambda b, t: (b, t, PJ_MLO // vw)),
            pl.BlockSpec((None, t_len, LANES), lambda b, t: (b, t, PJ_SMALL // LANES)),
            pl.BlockSpec((CONV_K, qkw2), lambda b, t: (0, 0)),
            pl.BlockSpec((1, LANES), lambda b, t: (0, 0)),
            pl.BlockSpec((1, vw), lambda b, t: (0, 0)),
        ],
        out_specs=pl.BlockSpec((None, t_len, vw), lambda b, t: (b, t, 0)),
        out_shape=jax.ShapeDtypeStruct((bsz, s_len, vw), BF16),
        scratch_shapes=[
            pltpu.VMEM((SUBLANES, qkw2), F32),
            pltpu.VMEM((t_len + SUBLANES, qkw2), F32),
            pltpu.VMEM((t_len, ML_HEADS * ML_QK), BF16),
            pltpu.VMEM((t_len, ML_HEADS * ML_QK), BF16),
            pltpu.VMEM((ML_HEADS, ML_QK, ML_V), F32),
            pltpu.VMEM((SUBLANES, ML_QK), F32),
            pltpu.VMEM((SUBLANES, LANES), F32),
        ],
        compiler_params=_cparams(("parallel", "arbitrary")),
    )(proj, proj, proj, proj, conv_w, bias_row, out_norm.reshape(1, vw))


def _ssd_body(xbc_ref, z_ref, sm_ref, conv_ref, convb_ref, dtb_ref, alog_ref, dexp_ref, norm_ref,
              expand_ref, y_ref, carry_ref, xpad_ref, act_s, dt_s, ys_s, st_ref):
    first = pl.program_id(1) == 0
    t_len = xbc_ref.shape[0]
    xw = SSM_HEADS * SSM_P
    gw = SSM_HPG * SSM_P

    @pl.when(first)
    def _():
        st_ref[...] = jnp.zeros_like(st_ref)

    act_s[...] = _silu(_causal_conv(xbc_ref, conv_ref, carry_ref, xpad_ref, first) + convb_ref[...])
    dt_s[...] = _softplus(sm_ref[...] + dtb_ref[...])
    a_row = -jnp.exp(alog_ref[...])

    causal = _tri(CHUNK)
    tri01 = causal.astype(BF16)
    expand = expand_ref[...]

    def chunk(c, carry):
        rows = slice(c * CHUNK, (c + 1) * CHUNK)
        dt = dt_s[rows, :]
        seg = _dot01_left(tri01, dt * a_row)
        seg_end = seg[CHUNK - 1:CHUNK, :]
        seg_t = seg.T
        dt_t = dt.T
        p_in = _dot01_right(jnp.exp(seg), expand)
        p_end = _dot01_right(jnp.exp(seg_end - seg) * dt, expand)
        dec = _dot01_right(jnp.broadcast_to(jnp.exp(seg_end), (SUBLANES, LANES)), expand)[0:1, :]
        x = act_s[rows, 0:xw]
        xwt = (x * p_end).astype(BF16)
        for g in range(SSM_G):
            bg = act_s[rows, xw + g * SSM_N:xw + (g + 1) * SSM_N].astype(BF16)
            cg = act_s[rows, xw + (SSM_G + g) * SSM_N:xw + (SSM_G + g + 1) * SSM_N].astype(BF16)
            gmat = _dot_nt(cg, bg)
            st = st_ref[g]
            y_in = _dot(cg, st.astype(BF16)) * p_in[:, g * gw:(g + 1) * gw]
            for hh in range(SSM_HPG):
                h = g * SSM_HPG + hh
                col = SM_DT + h
                seg_c = seg[:, col:col + 1]
                seg_r = seg_t[col:col + 1, :]
                dt_r = dt_t[col:col + 1, :]
                lmat = jnp.exp(jnp.where(causal, seg_c - seg_r, -jnp.inf))
                scores = (gmat * lmat * dt_r).astype(BF16)
                xh = x[:, h * SSM_P:(h + 1) * SSM_P]
                yh = _dot(scores, xh.astype(BF16)) + y_in[:, hh * SSM_P:(hh + 1) * SSM_P]
                ys_s[rows, h * SSM_P:(h + 1) * SSM_P] = yh + dexp_ref[:, h * SSM_P:(h + 1) * SSM_P] * xh
            st_ref[g] = dec[:, g * gw:(g + 1) * gw] * st + _dot_tn(bg, xwt[:, g * gw:(g + 1) * gw])
        return carry

    for c in range(t_len // CHUNK):
        chunk(c, 0)
    y = ys_s[...] * _silu(z_ref[...])
    y_ref[...] = (_rms(y) * norm_ref[...]).astype(y_ref.dtype)


def _ssd(proj, conv_w, conv_b, dtb_row, alog_row, d_exp, norm, expand, t_len=256):
    bsz, s_len, _ = proj.shape
    xw = SSM_HEADS * SSM_P
    cw = xw + 2 * SSM_G * SSM_N
    return pl.pallas_call(
        _ssd_body,
        grid=(bsz, s_len // t_len),
        in_specs=[
            pl.BlockSpec((None, t_len, cw), lambda b, t: (b, t, PJ_XBC // cw)),
            pl.BlockSpec((None, t_len, xw), lambda b, t: (b, t, PJ_SZ // xw)),
            pl.BlockSpec((None, t_len, LANES), lambda b, t: (b, t, PJ_SMALL // LANES)),
            pl.BlockSpec((CONV_K, cw), lambda b, t: (0, 0)),
            pl.BlockSpec((1, cw), lambda b, t: (0, 0)),
            pl.BlockSpec((1, LANES), lambda b, t: (0, 0)),
            pl.BlockSpec((1, LANES), lambda b, t: (0, 0)),
            pl.BlockSpec((1, xw), lambda b, t: (0, 0)),
            pl.BlockSpec((1, xw), lambda b, t: (0, 0)),
            pl.BlockSpec((LANES, xw), lambda b, t: (0, 0)),
        ],
        out_specs=pl.BlockSpec((None, t_len, xw), lambda b, t: (b, t, 0)),
        out_shape=jax.ShapeDtypeStruct((bsz, s_len, xw), BF16),
        scratch_shapes=[
            pltpu.VMEM((SUBLANES, cw), F32),
            pltpu.VMEM((t_len + SUBLANES, cw), F32),
            pltpu.VMEM((t_len, cw), F32),
            pltpu.VMEM((t_len, LANES), F32),
            pltpu.VMEM((t_len, xw), F32),
            pltpu.VMEM((SSM_G, SSM_N, SSM_HPG * SSM_P), F32),
        ],
        compiler_params=_cparams(("parallel", "arbitrary")),
    )(proj, proj, proj, conv_w, conv_b.reshape(1, cw), dtb_row, alog_row, d_exp, norm.reshape(1, xw), expand)


def _dsa_prep_body(dq_ref, dkv_ref, iq_ref, sm_ref, qn_ref, kn_ref, kvn_ref, wuk_ref, wuv_ref, ikn_ref,
                   qbd_ref, kd_ref, vt_ref, qit_ref, kidx_ref, widx_ref):
    t_len = dq_ref.shape[0]
    dq = dq_ref[...]
    q_scale = (DSA_HD ** -0.5) * LOG2_E
    qd = jnp.concatenate(
        [_rms(dq[:, h * DSA_HD:(h + 1) * DSA_HD]) * qn_ref[...] * q_scale for h in range(DSA_HEADS)],
        axis=1)
    ckv = (_rms(dkv_ref[...]) * kvn_ref[...]).astype(BF16)
    kraw = _dot(ckv, wuk_ref[...])
    kd_ref[...] = jnp.concatenate(
        [_rms(kraw[:, h * DSA_HD:(h + 1) * DSA_HD]) * kn_ref[...] for h in range(DSA_HEADS)],
        axis=1).astype(BF16)
    v_t = _t_blocks_rows(_dot(ckv, wuv_ref[...])).astype(BF16)
    ones = jnp.ones((V_ROWS - DSA_HD, t_len), BF16)
    vt_ref[...] = jnp.concatenate(
        [blk for h in range(DSA_HEADS) for blk in (v_t[h * DSA_HD:(h + 1) * DSA_HD], ones)], axis=0)
    sm = sm_ref[...]
    kidx_ref[...] = (_rms(sm[:, SM_IK:SM_IK + IDX_D]) * ikn_ref[...]).astype(BF16)
    iq = iq_ref[...] * (IDX_D ** -0.5)
    wsc = sm * (IDX_HEADS ** -0.5)
    zero = jnp.zeros((DSA_HD, Q_BLOCK), BF16)
    for j in range(t_len // Q_BLOCK):
        rows = slice(j * Q_BLOCK, (j + 1) * Q_BLOCK)
        qt = _t_blocks(qd[rows]).astype(BF16)
        for p in range(DSA_HEADS // 2):
            top = jnp.concatenate([qt[2 * p * DSA_HD:(2 * p + 1) * DSA_HD], zero], axis=1)
            bot = jnp.concatenate([zero, qt[(2 * p + 1) * DSA_HD:(2 * p + 2) * DSA_HD]], axis=1)
            qbd_ref[j, p] = jnp.concatenate([top, bot], axis=0)
        it = _t_blocks(iq[rows]).astype(BF16)
        qit_ref[j] = jnp.concatenate([it[h * IDX_D:(h + 1) * IDX_D] for h in range(IDX_HEADS)], axis=1)
        wt = wsc[rows].T
        widx_ref[j] = jnp.concatenate([wt[SM_IW + h:SM_IW + h + 1] for h in range(IDX_HEADS)], axis=1)


def _t_blocks_rows(x):
    rows = x.shape[0]
    return jnp.concatenate([_t_blocks(x[r:r + LANES]) for r in range(0, rows, LANES)], axis=1)


def _dsa_prep(proj, q_norm, k_norm, kv_norm, w_uk, w_uv, idx_k_norm, t_len=256):
    bsz, s_len, _ = proj.shape
    nq = s_len // Q_BLOCK
    qpb = t_len // Q_BLOCK
    bw = DSA_HEADS * DSA_HD
    iw = IDX_HEADS * IDX_D
    npair = DSA_HEADS // 2
    return pl.pallas_call(
        _dsa_prep_body,
        grid=(bsz, s_len // t_len),
        in_specs=[
            pl.BlockSpec((None, t_len, bw), lambda b, t: (b, t, PJ_DQ // bw)),
            pl.BlockSpec((None, t_len, DSA_LAT), lambda b, t: (b, t, PJ_DKV // DSA_LAT)),
            pl.BlockSpec((None, t_len, iw), lambda b, t: (b, t, PJ_IQ // iw)),
            pl.BlockSpec((None, t_len, LANES), lambda b, t: (b, t, PJ_SMALL // LANES)),
            pl.BlockSpec((1, DSA_HD), lambda b, t: (0, 0)),
            pl.BlockSpec((1, DSA_HD), lambda b, t: (0, 0)),
            pl.BlockSpec((1, DSA_LAT), lambda b, t: (0, 0)),
            pl.BlockSpec((DSA_LAT, bw), lambda b, t: (0, 0)),
            pl.BlockSpec((DSA_LAT, bw), lambda b, t: (0, 0)),
            pl.BlockSpec((1, IDX_D), lambda b, t: (0, 0)),
        ],
        out_specs=[
            pl.BlockSpec((None, qpb, npair, 2 * DSA_HD, 2 * Q_BLOCK), lambda b, t: (b, t, 0, 0, 0)),
            pl.BlockSpec((None, t_len, bw), lambda b, t: (b, t, 0)),
            pl.BlockSpec((None, DSA_HEADS * V_ROWS, t_len), lambda b, t: (b, 0, t)),
            pl.BlockSpec((None, qpb, IDX_D, IDX_HEADS * Q_BLOCK), lambda b, t: (b, t, 0, 0)),
            pl.BlockSpec((None, t_len, IDX_D), lambda b, t: (b, t, 0)),
            pl.BlockSpec((None, qpb, 1, IDX_HEADS * Q_BLOCK), lambda b, t: (b, t, 0, 0)),
        ],
        out_shape=[
            jax.ShapeDtypeStruct((bsz, nq, npair, 2 * DSA_HD, 2 * Q_BLOCK), BF16),
            jax.ShapeDtypeStruct((bsz, s_len, bw), BF16),
            jax.ShapeDtypeStruct((bsz, DSA_HEADS * V_ROWS, s_len), BF16),
            jax.ShapeDtypeStruct((bsz, nq, IDX_D, IDX_HEADS * Q_BLOCK), BF16),
            jax.ShapeDtypeStruct((bsz, s_len, IDX_D), BF16),
            jax.ShapeDtypeStruct((bsz, nq, 1, IDX_HEADS * Q_BLOCK), F32),
        ],
        compiler_params=_cparams(("parallel", "parallel")),
    )(proj, proj, proj, proj, q_norm.reshape(1, DSA_HD), k_norm.reshape(1, DSA_HD),
      kv_norm.reshape(1, DSA_LAT), w_uk, w_uv, idx_k_norm.reshape(1, IDX_D))


def _dsa_body(qtab, ktab, ltab, kidx_ref, qit_ref, widx_ref, qbd_ref, kd_ref, vt_ref, y_ref,
              keys_ref, gmax_ref, thr_ref, cnt_ref, m_ref, acc_ref, *pair_refs, tk, top, sub_a, sub_b, sub_c):
    lg_refs = pair_refs[:DSA_HEADS // 2]
    pb_refs = pair_refs[DSA_HEADS // 2:]
    step = pl.program_id(1)
    qi = qtab[step]
    kj = ktab[step]
    n_keys_tiles = ltab[step] + 1
    pair_w = 2 * Q_BLOCK

    @pl.when(kj == 0)
    def _():
        lane = lax.broadcasted_iota(I32, (1, Q_BLOCK), 1)
        limit = qi * Q_BLOCK + (lane // CHUNK + 1) * CHUNK
        row = lax.broadcasted_iota(I32, (sub_a, Q_BLOCK), 0)

        def score_tile(t, carry):
            for c in range(tk // sub_a):
                r0 = pl.multiple_of(t * tk + c * sub_a, sub_a)
                kt = kidx_ref[pl.ds(r0, sub_a), :]
                acc2 = jnp.zeros((sub_a, pair_w), F32)
                for p in range(IDX_HEADS // 2):
                    cols = slice(p * pair_w, (p + 1) * pair_w)
                    acc2 = acc2 + jnp.maximum(_dot(kt, qit_ref[:, cols]), 0.0) * widx_ref[:, cols]
                acc = acc2[:, :Q_BLOCK] + acc2[:, Q_BLOCK:]
                score = jnp.where(row + r0 < limit, acc, -jnp.inf)
                bits = lax.bitcast_convert_type(score, I32)
                key = jnp.where(bits < 0, bits ^ jnp.int32(0x7FFFFFFF), bits)
                keys_ref[pl.ds(r0, sub_a), :] = key
                g0 = (c * sub_a) % N_GROUPS
                gmax_ref[g0:g0 + sub_a, :] = jnp.maximum(gmax_ref[g0:g0 + sub_a, :], key)
            return carry

        gmax_ref[...] = jnp.full_like(gmax_ref, INT_MIN)
        lax.fori_loop(0, n_keys_tiles, score_tile, 0)

        gm = gmax_ref[...]
        k_hi = jnp.max(gm, axis=0, keepdims=True)
        k_lo = jnp.min(gm, axis=0, keepdims=True)
        n_bits = jnp.int32(32) - jnp.min(lax.clz(k_hi ^ k_lo))
        low_mask = jnp.where(n_bits >= 32, jnp.int32(-1), lax.shift_left(jnp.int32(1), n_bits) - 1)
        t_init = ((k_hi ^ INT_MIN) & ~low_mask) ^ INT_MIN
        n_acc = 4 * SUBLANES
        n_b_tiles = n_keys_tiles * (tk // sub_b)

        def count_ge(cand):
            def count_tile(t, cnt):
                r0 = pl.multiple_of(t * sub_b, sub_b)
                ge = keys_ref[pl.ds(r0, sub_b), :] >= cand
                ones = jnp.where(ge, jnp.int32(1), jnp.int32(0))
                return cnt + jnp.sum(ones.reshape(sub_b // n_acc, n_acc, Q_BLOCK), axis=0)

            cnt = lax.fori_loop(0, n_b_tiles, count_tile, jnp.zeros((n_acc, Q_BLOCK), I32))
            return jnp.sum(cnt, axis=0, keepdims=True)

        def bit_step(i, state):
            thr, at_thr = state
            cand = thr + lax.shift_left(jnp.int32(1), n_bits - 1 - i)
            total = count_ge(cand)
            keep = total >= top
            return jnp.where(keep, cand, thr), jnp.where(keep, total, at_thr)

        thr, at_thr = lax.fori_loop(0, n_bits, bit_step, (t_init, jnp.full((1, Q_BLOCK), -1, I32)))
        thr_ref[...] = jnp.broadcast_to(thr, thr_ref.shape)
        cnt_ref[...] = jnp.broadcast_to(at_thr, cnt_ref.shape)

        @pl.when(jnp.min(at_thr) < 0)
        def _():
            t = thr_ref[0:1, :]
            cnt_ref[...] = jnp.broadcast_to(count_ge(t), cnt_ref.shape)

        surplus = jnp.where(thr > NEG_INF_KEY, cnt_ref[0:1, :], 0)

        @pl.when(jnp.max(surplus) > top)
        def _():
            t = thr_ref[0:1, :]
            need = top - count_ge(t + 1)

            row = lax.broadcasted_iota(I32, (SUBLANES, Q_BLOCK), 0)

            def fix_rows(g, seen):
                r0 = pl.multiple_of(g * SUBLANES, SUBLANES)
                k = keys_ref[pl.ds(r0, SUBLANES), :]
                tie = k == t
                run = jnp.where(tie, jnp.int32(1), jnp.int32(0))
                for sh in (1, 2, 4):
                    run = run + jnp.where(row >= sh, pltpu.roll(run, sh, 0), 0)
                keys_ref[pl.ds(r0, SUBLANES), :] = jnp.where(jnp.logical_and(tie, seen + run > need), k - 1, k)
                return seen + run[SUBLANES - 1:SUBLANES, :]

            lax.fori_loop(0, n_keys_tiles * (tk // SUBLANES), fix_rows, jnp.zeros((1, Q_BLOCK), I32))

        thr = jnp.maximum(thr, jnp.int32(NEG_INF_KEY + 1))
        thr_ref[...] = jnp.broadcast_to(thr, thr_ref.shape)
        m_ref[...] = jnp.full_like(m_ref, NEG_BIG)
        acc_ref[...] = jnp.zeros_like(acc_ref)

    k0 = pl.multiple_of(kj * tk, tk)
    npair = DSA_HEADS // 2
    thr = thr_ref[0:1, :]

    def logits(p):
        lg_refs[p][...] = _dot(kd_ref[:, p * 2 * DSA_HD:(p + 1) * 2 * DSA_HD], qbd_ref[p])

    def mask_pair(p):
        out = [jnp.full((SUBLANES, Q_BLOCK), NEG_BIG, F32) for _ in range(2)]
        for c in range(tk // sub_c):
            rows = slice(c * sub_c, (c + 1) * sub_c)
            sel = keys_ref[pl.ds(k0 + c * sub_c, sub_c), :] >= thr
            for j in range(2):
                qs = slice(j * Q_BLOCK, (j + 1) * Q_BLOCK)
                lg = jnp.where(sel, lg_refs[p][rows, qs], NEG_BIG)
                lg_refs[p][rows, qs] = lg
                out[j] = jnp.maximum(out[j], jnp.max(lg.reshape(sub_c // SUBLANES, SUBLANES, Q_BLOCK), axis=0))
        return out

    def values(p, mx):
        m_old = m_ref[p:p + 1, :]
        tile_max = jnp.concatenate([jnp.max(mx[j], axis=0, keepdims=True) for j in range(2)], axis=1)
        m_new = jnp.maximum(m_old, tile_max)
        alpha = jnp.exp2(m_old - m_new)
        m_ref[p:p + 1, :] = m_new
        pb_refs[p][...] = jnp.exp2(lg_refs[p][...] - m_new).astype(BF16)
        for j in range(2):
            h = 2 * p + j
            qs = slice(j * Q_BLOCK, (j + 1) * Q_BLOCK)
            acc_ref[h] = (alpha[:, qs] * acc_ref[h]
                          + _dot(vt_ref[h * V_ROWS:(h + 1) * V_ROWS, :], pb_refs[p][:, qs]))

    logits(0)
    logits(1)
    mx0 = mask_pair(0)
    logits(2)
    mx1 = mask_pair(1)
    values(0, mx0)
    logits(3)
    mx2 = mask_pair(2)
    values(1, mx1)
    mx3 = mask_pair(3)
    values(2, mx2)
    values(3, mx3)

    @pl.when(kj == ltab[step])
    def _():
        for h in range(DSA_HEADS):
            acc = acc_ref[h]
            o_t = acc[:DSA_HD] / acc[DSA_HD:DSA_HD + 1]
            y_ref[:, h * DSA_HD:(h + 1) * DSA_HD] = o_t.T.astype(y_ref.dtype)


def _dsa_tables(s_len, tk):
    qs, ks, ls = [], [], []
    for qi in range(s_len // Q_BLOCK):
        last = (qi * Q_BLOCK + Q_BLOCK - 1) // tk
        for kj in range(last + 1):
            qs.append(qi)
            ks.append(kj)
            ls.append(last)
    return (np.asarray(qs, np.int32), np.asarray(ks, np.int32), np.asarray(ls, np.int32))


def _dsa(qbd, kd, vt, qit, kidx, widx, tk=1024, sub_a=128, sub_b=1024, sub_c=128):
    bsz, s_len, bw = kd.shape
    npair = DSA_HEADS // 2
    tk = min(tk, s_len)
    top = min(INDEX_TOPK, s_len // 4)
    qtab, ktab, ltab = _dsa_tables(s_len, tk)
    grid_spec = pltpu.PrefetchScalarGridSpec(
        num_scalar_prefetch=3,
        grid=(bsz, len(qtab)),
        in_specs=[
            pl.BlockSpec((None, s_len, IDX_D), lambda b, s, qt, kt, lt: (b, 0, 0)),
            pl.BlockSpec((None, None, IDX_D, IDX_HEADS * Q_BLOCK), lambda b, s, qt, kt, lt: (b, qt[s], 0, 0)),
            pl.BlockSpec((None, None, 1, IDX_HEADS * Q_BLOCK), lambda b, s, qt, kt, lt: (b, qt[s], 0, 0)),
            pl.BlockSpec((None, None, npair, 2 * DSA_HD, 2 * Q_BLOCK), lambda b, s, qt, kt, lt: (b, qt[s], 0, 0, 0)),
            pl.BlockSpec((None, tk, bw), lambda b, s, qt, kt, lt: (b, kt[s], 0)),
            pl.BlockSpec((None, DSA_HEADS * V_ROWS, tk), lambda b, s, qt, kt, lt: (b, 0, kt[s])),
        ],
        out_specs=pl.BlockSpec((None, Q_BLOCK, bw), lambda b, s, qt, kt, lt: (b, qt[s], 0)),
        scratch_shapes=[
            pltpu.VMEM((s_len, Q_BLOCK), I32),
            pltpu.VMEM((N_GROUPS, Q_BLOCK), I32),
            pltpu.VMEM((SUBLANES, Q_BLOCK), I32),
            pltpu.VMEM((SUBLANES, Q_BLOCK), I32),
            pltpu.VMEM((SUBLANES, 2 * Q_BLOCK), F32),
            pltpu.VMEM((DSA_HEADS, V_ROWS, Q_BLOCK), F32),
        ] + [pltpu.VMEM((tk, 2 * Q_BLOCK), F32) for _ in range(npair)] + [
            pltpu.VMEM((tk, 2 * Q_BLOCK), BF16) for _ in range(npair)
        ],
    )
    return pl.pallas_call(
        functools.partial(_dsa_body, tk=tk, top=top, sub_a=sub_a, sub_b=min(sub_b, tk), sub_c=sub_c),
        grid_spec=grid_spec,
        out_shape=jax.ShapeDtypeStruct((bsz, s_len, bw), BF16),
        compiler_params=_cparams(("parallel", "arbitrary")),
    )(jnp.asarray(qtab), jnp.asarray(ktab), jnp.asarray(ltab), kidx, qit, widx, qbd, kd, vt)


def _merge_body(x_ref, g0_ref, g1_ref, g2_ref, ya_ref, yb_ref, yc_ref, wbr_ref, wout_ref, o_ref):
    @pl.when(pl.program_id(1) == 0)
    def _():
        o_ref[...] = x_ref[...]

    merged = (_sigmoid(g0_ref[...]) * _dot(ya_ref[...], wbr_ref[0])
              + _sigmoid(g1_ref[...]) * _dot(yb_ref[...], wbr_ref[1])
              + _sigmoid(g2_ref[...]) * _dot(yc_ref[...], wbr_ref[2]))
    o_ref[...] += _dot(merged.astype(BF16), wout_ref[...])


def _merge(x2, proj2, ya, yb, yc, w_branch, w_out, tm=512, tn=512):
    m, d = x2.shape
    bw = ya.shape[1]
    nj = d // tn
    gate_spec = lambda b: pl.BlockSpec((tm, tn), lambda i, j: (i, PJ_GATES // tn + b * nj + j))
    y_spec = pl.BlockSpec((tm, bw), lambda i, j: (i, 0))
    return pl.pallas_call(
        _merge_body,
        grid=(m // tm, nj),
        in_specs=[
            pl.BlockSpec((tm, d), lambda i, j: (i, 0)),
            gate_spec(0), gate_spec(1), gate_spec(2),
            y_spec, y_spec, y_spec,
            pl.BlockSpec((N_BRANCH, bw, tn), lambda i, j: (0, 0, j)),
            pl.BlockSpec((tn, d), lambda i, j: (j, 0)),
        ],
        out_specs=pl.BlockSpec((tm, d), lambda i, j: (i, 0)),
        out_shape=jax.ShapeDtypeStruct((m, d), F32),
        compiler_params=_cparams(("parallel", "arbitrary")),
    )(x2, proj2, proj2, proj2, ya, yb, yc, w_branch, w_out)


def _xattn_body(x_ref, g_ref, wq_ref, kv_ref, qn_ref, kn_ref, wo_ref, o_ref):
    x = x_ref[...]
    xn = (_rms(x) * g_ref[...]).astype(BF16)
    q = _dot(xn, wq_ref[...])
    kv = kv_ref[...]
    xw = XA_HEADS * XA_HD
    outs = []
    for h in range(XA_HEADS):
        hs = slice(h * XA_HD, (h + 1) * XA_HD)
        qh = (_rms(q[:, hs]) * qn_ref[...] * (XA_HD ** -0.5)).astype(BF16)
        kh = (_rms(kv[:, hs]) * kn_ref[...]).astype(BF16)
        vh = kv[:, xw + h * XA_HD:xw + (h + 1) * XA_HD].astype(BF16)
        s = _dot_nt(qh, kh)
        p = jnp.exp(s - jnp.max(s, axis=1, keepdims=True))
        p = p / jnp.sum(p, axis=1, keepdims=True)
        outs.append(_dot(p.astype(BF16), vh))
    o = jnp.concatenate(outs, axis=1).astype(BF16)
    o_ref[...] = x + _dot(o, wo_ref[...])


def _xattn(x, g, wq, kv, q_norm, k_norm, wo, tm=512):
    bsz, s_len, d = x.shape
    n_mem = kv.shape[1]
    xw = XA_HEADS * XA_HD
    return pl.pallas_call(
        _xattn_body,
        grid=(bsz, s_len // tm),
        in_specs=[
            pl.BlockSpec((None, tm, d), lambda b, i: (b, i, 0)),
            pl.BlockSpec((1, d), lambda b, i: (0, 0)),
            pl.BlockSpec((d, xw), lambda b, i: (0, 0)),
            pl.BlockSpec((None, n_mem, 2 * xw), lambda b, i: (b, 0, 0)),
            pl.BlockSpec((1, XA_HD), lambda b, i: (0, 0)),
            pl.BlockSpec((1, XA_HD), lambda b, i: (0, 0)),
            pl.BlockSpec((xw, d), lambda b, i: (0, 0)),
        ],
        out_specs=pl.BlockSpec((None, tm, d), lambda b, i: (b, i, 0)),
        out_shape=jax.ShapeDtypeStruct((bsz, s_len, d), F32),
        compiler_params=_cparams(("parallel", "parallel")),
    )(x, g.reshape(1, d), wq, kv, q_norm.reshape(1, XA_HD), k_norm.reshape(1, XA_HD), wo)


def _small_row(entries):
    row = jnp.zeros((1, LANES), F32)
    for off, vec in entries:
        row = row.at[0, off:off + vec.shape[0]].set(vec.astype(F32))
    return row


def _mixer(x2, bsz, mix_norm, w_in, ml_conv, ml_i_bias, ml_f_bias, ml_out_norm,
           dsa_q_norm, dsa_k_norm, dsa_kv_norm, dsa_w_uk, dsa_w_uv, idx_k_norm,
           ssm_conv, ssm_conv_b, ssm_dt_bias, ssm_a_log, ssm_d, ssm_norm, w_branch, w_out):
    m, d = x2.shape
    s_len = m // bsz
    proj2 = _proj(x2, mix_norm, _reorder_w_in(w_in), tn=1024)
    proj = proj2.reshape(bsz, s_len, PJ_WIDTH)

    ya = _mlstm(proj, ml_conv, _small_row([(SM_MI, ml_i_bias), (SM_MF, ml_f_bias)]), ml_out_norm)

    head_of_lane = np.arange(SSM_HEADS * SSM_P) // SSM_P
    expand = jnp.asarray(np.arange(LANES)[:, None] == (SM_DT + head_of_lane)[None, :], BF16)
    yc = _ssd(proj, ssm_conv, ssm_conv_b, _small_row([(SM_DT, ssm_dt_bias)]),
              _small_row([(SM_DT, ssm_a_log)]), jnp.repeat(ssm_d, SSM_P).reshape(1, -1), ssm_norm, expand)

    qbd, kd, vt, qit, kidx, widx = _dsa_prep(proj, dsa_q_norm, dsa_k_norm, dsa_kv_norm,
                                             dsa_w_uk.astype(BF16), dsa_w_uv.astype(BF16), idx_k_norm)
    yb = _dsa(qbd, kd, vt, qit, kidx, widx)

    bw = ya.shape[-1]
    return _merge(x2, proj2, ya.reshape(m, bw), yb.reshape(m, bw), yc.reshape(m, bw),
                  w_branch.astype(BF16), w_out.astype(BF16))


def kernel(x, mem, ffn1_norm, ffn1_w_up, ffn1_w_down, mix_norm, w_in, ml_conv, ml_i_bias, ml_f_bias, ml_out_norm, dsa_q_norm, dsa_k_norm, dsa_kv_norm, dsa_w_uk, dsa_w_uv, idx_k_norm, ssm_conv, ssm_conv_b, ssm_dt_bias, ssm_a_log, ssm_d, ssm_norm, w_branch, w_out, xa_norm, xa_mem_norm, xa_wq, xa_wkv, xa_q_norm, xa_k_norm, xa_wo, ffn2_norm, ffn2_w_up, ffn2_w_down):
    bsz, s_len, d = x.shape
    n_mem = mem.shape[1]
    depth = w_in.shape[0]
    x2 = x.reshape(bsz * s_len, d)
    mem2 = mem.reshape(bsz * n_mem, d)
    for l in range(depth):
        x2 = _ffn(x2, ffn1_norm[l], ffn1_w_up[l].astype(BF16), ffn1_w_down[l].astype(BF16))
        x2 = _mixer(x2, bsz, mix_norm[l], w_in[l], ml_conv[l], ml_i_bias[l], ml_f_bias[l], ml_out_norm[l],
                    dsa_q_norm[l], dsa_k_norm[l], dsa_kv_norm[l], dsa_w_uk[l], dsa_w_uv[l], idx_k_norm[l],
                    ssm_conv[l], ssm_conv_b[l], ssm_dt_bias[l], ssm_a_log[l], ssm_d[l], ssm_norm[l],
                    w_branch[l], w_out[l])
        kv = _proj(mem2, xa_mem_norm[l], xa_wkv[l].astype(BF16)).reshape(bsz, n_mem, -1)
        x3 = _xattn(x2.reshape(bsz, s_len, d), xa_norm[l], xa_wq[l].astype(BF16), kv,
                    xa_q_norm[l], xa_k_norm[l], xa_wo[l].astype(BF16))
        x2 = _ffn(x3.reshape(bsz * s_len, d), ffn2_norm[l], ffn2_w_up[l].astype(BF16), ffn2_w_down[l].astype(BF16))
    return x2.reshape(bsz, s_len, d)
```

```python
import functools

import numpy as np
import jax
import jax.numpy as jnp
from jax import lax
from jax.experimental import pallas as pl
from jax.experimental.pallas import tpu as pltpu

F32 = jnp.float32
BF16 = jnp.bfloat16
I32 = jnp.int32

EPS = 1e-6
CHUNK = 64
CONV_K = 4
LANES = 128
SUBLANES = 8
VMEM_LIMIT_BYTES = 56 * 1024 * 1024

ML_HEADS, ML_QK, ML_V = 4, 128, 256
DSA_HEADS, DSA_HD, DSA_LAT = 8, 128, 512
IDX_HEADS, IDX_D, INDEX_TOPK = 16, 64, 256
Q_BLOCK = 128
SEARCH_GROUP = 6
N_GROUPS = 256
V_ROWS = DSA_HD + 16
SSM_HEADS, SSM_P, SSM_G, SSM_N = 16, 64, 2, 128
SSM_HPG = SSM_HEADS // SSM_G
XA_HEADS, XA_HD = 4, 128
N_BRANCH = 3

SM_IK, SM_IW, SM_DT, SM_MI, SM_MF = 0, 64, 80, 96, 100

NEG_BIG = -1e30
LOG2_E = 1.4426950408889634
INT_MIN = -2 ** 31
NEG_INF_KEY = -2139095041


def _cparams(sem):
    return pltpu.CompilerParams(dimension_semantics=sem, vmem_limit_bytes=VMEM_LIMIT_BYTES)


def _rms(x):
    return x * lax.rsqrt(jnp.mean(x * x, axis=-1, keepdims=True) + EPS)


def _sigmoid(x):
    return 1.0 / (1.0 + jnp.exp(-x))


def _silu(x):
    return x * _sigmoid(x)


def _softplus(x):
    return jnp.maximum(x, 0.0) + jnp.log(1.0 + jnp.exp(-jnp.abs(x)))


def _dot(a, b):
    return jnp.dot(a, b, preferred_element_type=F32)


def _dot_nt(a, b):
    return lax.dot_general(a, b, (((1,), (1,)), ((), ())), preferred_element_type=F32)


def _dot_tn(a, b):
    return lax.dot_general(a, b, (((0,), (0,)), ((), ())), preferred_element_type=F32)


def _split3(x):
    hi = x.astype(BF16)
    r = x - hi.astype(F32)
    mid = r.astype(BF16)
    lo = (r - mid.astype(F32)).astype(BF16)
    return hi, mid, lo


def _dot01_left(m01, x):
    hi, mid, lo = _split3(x)
    return _dot(m01, hi) + _dot(m01, mid) + _dot(m01, lo)


def _dot01_right(x, m01):
    hi, mid, lo = _split3(x)
    return _dot(hi, m01) + _dot(mid, m01) + _dot(lo, m01)


def _tri(n):
    r = lax.broadcasted_iota(I32, (n, n), 0)
    c = lax.broadcasted_iota(I32, (n, n), 1)
    return r >= c


def _t_blocks(x):
    cols = x.shape[1]
    return jnp.concatenate([x[:, c:c + LANES].T for c in range(0, cols, LANES)], axis=0)


def _ffn_body(x_ref, g_ref, wa_ref, wb_ref, wd_ref, o_ref, xn_ref):
    @pl.when(pl.program_id(1) == 0)
    def _():
        x = x_ref[...]
        xn_ref[...] = (_rms(x) * g_ref[...]).astype(BF16)
        o_ref[...] = x

    xn = xn_ref[...]
    a = _dot(xn, wa_ref[...])
    b = _dot(xn, wb_ref[...])
    act = (0.5 * _silu(a) * b).astype(BF16)
    o_ref[...] += _dot(act, wd_ref[...])


def _ffn(x2, g, w_up, w_down, tm=1024, tf=512):
    m, d = x2.shape
    dff = w_down.shape[0]
    nf = dff // tf
    return pl.pallas_call(
        _ffn_body,
        grid=(m // tm, nf),
        in_specs=[
            pl.BlockSpec((tm, d), lambda i, f: (i, 0)),
            pl.BlockSpec((1, d), lambda i, f: (0, 0)),
            pl.BlockSpec((d, tf), lambda i, f: (0, f)),
            pl.BlockSpec((d, tf), lambda i, f: (0, nf + f)),
            pl.BlockSpec((tf, d), lambda i, f: (f, 0)),
        ],
        out_specs=pl.BlockSpec((tm, d), lambda i, f: (i, 0)),
        out_shape=jax.ShapeDtypeStruct((m, d), F32),
        scratch_shapes=[pltpu.VMEM((tm, d), BF16)],
        compiler_params=_cparams(("parallel", "arbitrary")),
    )(x2, g.reshape(1, d), w_up, w_up, w_down)


def _proj_body(x_ref, g_ref, w_ref, o_ref, xn_ref):
    @pl.when(pl.program_id(1) == 0)
    def _():
        xn_ref[...] = (_rms(x_ref[...]) * g_ref[...]).astype(BF16)

    o_ref[...] = _dot(xn_ref[...], w_ref[...])


def _proj(x2, g, w, tm=1024, tn=512):
    m, d = x2.shape
    n = w.shape[1]
    tm = min(tm, m)
    return pl.pallas_call(
        _proj_body,
        grid=(m // tm, n // tn),
        in_specs=[
            pl.BlockSpec((tm, d), lambda i, j: (i, 0)),
            pl.BlockSpec((1, d), lambda i, j: (0, 0)),
            pl.BlockSpec((d, tn), lambda i, j: (0, j)),
        ],
        out_specs=pl.BlockSpec((tm, tn), lambda i, j: (i, j)),
        out_shape=jax.ShapeDtypeStruct((m, n), F32),
        scratch_shapes=[pltpu.VMEM((tm, d), BF16)],
        compiler_params=_cparams(("parallel", "arbitrary")),
    )(x2, g.reshape(1, d), w)


PJ_GATES = 0
PJ_MLQK = 6144
PJ_MLV = 7168
PJ_MLO = 8192
PJ_DQ = 9216
PJ_IQ = 10240
PJ_SZ = 11264
PJ_XBC = 12288
PJ_DKV = 13824
PJ_SMALL = 14336
PJ_WIDTH = 15360


def _reorder_w_in(w_in):
    d = w_in.shape[0]
    sizes = (512, 512, 1024, 1024, 4, 4, 1024, 512, 1024, 64, 16, 1024, 1536, 16, 6144)
    offs = np.concatenate([[0], np.cumsum(sizes)])
    (ml_q, ml_k, ml_v, ml_o, ml_i, ml_f, d_q, d_kv, i_q, i_k, i_w, s_z, s_xbc, s_dt, gates) = [
        w_in[:, int(offs[j]):int(offs[j + 1])] for j in range(len(sizes))]
    small = jnp.concatenate(
        [i_k, i_w, s_dt, ml_i, ml_f, jnp.zeros((d, LANES - 104), w_in.dtype)], axis=1)
    pad = jnp.zeros((d, PJ_WIDTH - PJ_SMALL - LANES), w_in.dtype)
    return jnp.concatenate(
        [gates, ml_q, ml_k, ml_v, ml_o, d_q, i_q, s_z, s_xbc, d_kv, small, pad], axis=1).astype(BF16)


def _causal_conv(x_ref, w_ref, carry_ref, xpad_ref, first):
    t_len = x_ref.shape[0]

    @pl.when(first)
    def _():
        carry_ref[...] = jnp.zeros_like(carry_ref)

    xpad_ref[0:SUBLANES, :] = carry_ref[...]
    xpad_ref[SUBLANES:SUBLANES + t_len, :] = x_ref[...]
    carry_ref[...] = x_ref[t_len - SUBLANES:t_len, :]
    acc = None
    for j in range(CONV_K):
        start = SUBLANES - (CONV_K - 1) + j
        term = w_ref[j:j + 1, :] * xpad_ref[start:start + t_len, :]
        acc = term if acc is None else acc + term
    return acc


def _mlstm_body(qk_ref, v_ref, o_ref, sm_ref, conv_ref, bias_ref, onorm_ref, y_ref,
                carry_ref, xpad_ref, q_s, k_s, ct_ref, n_ref, m_ref):
    first = pl.program_id(1) == 0
    t_len = qk_ref.shape[0]
    qkw = ML_HEADS * ML_QK

    @pl.when(first)
    def _():
        ct_ref[...] = jnp.zeros_like(ct_ref)
        n_ref[...] = jnp.zeros_like(n_ref)
        m_ref[...] = jnp.zeros_like(m_ref)

    qk = _silu(_causal_conv(qk_ref, conv_ref, carry_ref, xpad_ref, first))
    q_s[...] = (qk[:, :qkw] * (ML_QK ** -0.5)).astype(BF16)
    k_s[...] = qk[:, qkw:].astype(BF16)

    causal = _tri(CHUNK)
    tri01 = causal.astype(BF16)

    def chunk(c, carry):
        rows = slice(c * CHUNK, (c + 1) * CHUNK)
        sm = sm_ref[rows, :] + bias_ref[...]
        lf = -_softplus(-sm)
        b_col = _dot01_left(tri01, lf)
        b_t = b_col.T
        s_t = sm.T
        for h in range(ML_HEADS):
            bc = b_col[:, SM_MF + h:SM_MF + h + 1]
            ic = sm[:, SM_MI + h:SM_MI + h + 1]
            br = b_t[SM_MF + h:SM_MF + h + 1, :]
            ir = s_t[SM_MI + h:SM_MI + h + 1, :]
            m_old = m_ref[h:h + 1, 0:1]
            dmat = jnp.where(causal, bc - br + ir, -jnp.inf)
            inter = bc + m_old
            m_t = jnp.maximum(inter, jnp.max(dmat, axis=1, keepdims=True))
            qh = q_s[rows, h * ML_QK:(h + 1) * ML_QK]
            kh = k_s[rows, h * ML_QK:(h + 1) * ML_QK]
            vh = v_ref[rows, h * ML_V:(h + 1) * ML_V].astype(BF16)
            s = _dot_nt(qh, kh) * jnp.exp(dmat - m_t)
            decay = jnp.exp(inter - m_t)
            ct = ct_ref[h]
            nvec = n_ref[h:h + 1, :]
            num = _dot(s.astype(BF16), vh) + decay * _dot(qh, ct.astype(BF16))
            den = (jnp.sum(s, axis=1, keepdims=True)
                   + decay * jnp.sum(qh.astype(F32) * nvec, axis=1, keepdims=True))
            hout = num / jnp.maximum(jnp.abs(den), jnp.exp(-m_t))
            hn = _rms(hout) * onorm_ref[:, h * ML_V:(h + 1) * ML_V]
            og = _sigmoid(o_ref[rows, h * ML_V:(h + 1) * ML_V])
            y_ref[rows, h * ML_V:(h + 1) * ML_V] = (og * hn).astype(y_ref.dtype)
            b_end = bc[CHUNK - 1:CHUNK, :]
            g_r = b_end - br + ir
            g_c = b_end - bc + ic
            m_new = jnp.maximum(b_end + m_old, jnp.max(g_r, axis=1, keepdims=True))
            wk_c = jnp.exp(g_c - m_new)
            sdec = jnp.exp(b_end + m_old - m_new)
            kw = kh.astype(F32) * wk_c
            ct_ref[h] = sdec * ct + _dot_tn(kw.astype(BF16), vh)
            n_ref[h:h + 1, :] = sdec * nvec + jnp.sum(kw, axis=0, keepdims=True)
            m_ref[h:h + 1, :] = jnp.broadcast_to(m_new, (1, LANES))
        return carry

    for c in range(t_len // CHUNK):
        chunk(c, 0)


def _mlstm(proj, conv_w, bias_row, out_norm, t_len=256):
    bsz, s_len, _ = proj.shape
    vw = ML_HEADS * ML_V
    qkw2 = 2 * ML_HEADS * ML_QK
    return pl.pallas_call(
        _mlstm_body,
        grid=(bsz, s_len // t_len),
        in_specs=[
            pl.BlockSpec((None, t_len, qkw2), lambda b, t: (b, t, PJ_MLQK // qkw2)),
            pl.BlockSpec((None, t_len, vw), lambda b, t: (b, t, PJ_MLV // vw)),
            pl.BlockSpec((None, t_len, vw), lambda b, t: (b, t, PJ_MLO // vw)),
            pl.BlockSpec((None, t_len, LANES), lambda b, t: (b, t, PJ_SMALL // LANES)),
            pl.BlockSpec((CONV_K, qkw2), lambda b, t: (0, 0)),
            pl.BlockSpec((1, LANES), lambda b, t: (0, 0)),
            pl.BlockSpec((1, vw), lambda b, t: (0, 0)),
        ],
        out_specs=pl.BlockSpec((None, t_len, vw), lambda b, t: (b, t, 0)),
        out_shape=jax.ShapeDtypeStruct((bsz, s_len, vw), BF16),
        scratch_shapes=[
            pltpu.VMEM((SUBLANES, qkw2), F32),
            pltpu.VMEM((t_len + SUBLANES, qkw2), F32),
            pltpu.VMEM((t_len, ML_HEADS * ML_QK), BF16),
            pltpu.VMEM((t_len, ML_HEADS * ML_QK), BF16),
            pltpu.VMEM((ML_HEADS, ML_QK, ML_V), F32),
            pltpu.VMEM((SUBLANES, ML_QK), F32),
            pltpu.VMEM((SUBLANES, LANES), F32),
        ],
        compiler_params=_cparams(("parallel", "arbitrary")),
    )(proj, proj, proj, proj, conv_w, bias_row, out_norm.reshape(1, vw))


def _ssd_body(xbc_ref, z_ref, sm_ref, conv_ref, convb_ref, dtb_ref, alog_ref, dexp_ref, norm_ref,
              expand_ref, y_ref, carry_ref, xpad_ref, act_s, dt_s, ys_s, st_ref):
    first = pl.program_id(1) == 0
    t_len = xbc_ref.shape[0]
    xw = SSM_HEADS * SSM_P
    gw = SSM_HPG * SSM_P

    @pl.when(first)
    def _():
        st_ref[...] = jnp.zeros_like(st_ref)

    act_s[...] = _silu(_causal_conv(xbc_ref, conv_ref, carry_ref, xpad_ref, first) + convb_ref[...])
    dt_s[...] = _softplus(sm_ref[...] + dtb_ref[...])
    a_row = -jnp.exp(alog_ref[...])

    causal = _tri(CHUNK)
    tri01 = causal.astype(BF16)
    expand = expand_ref[...]

    def chunk(c, carry):
        rows = slice(c * CHUNK, (c + 1) * CHUNK)
        dt = dt_s[rows, :]
        seg = _dot01_left(tri01, dt * a_row)
        seg_end = seg[CHUNK - 1:CHUNK, :]
        seg_t = seg.T
        dt_t = dt.T
        p_in = _dot01_right(jnp.exp(seg), expand)
        p_end = _dot01_right(jnp.exp(seg_end - seg) * dt, expand)
        dec = _dot01_right(jnp.broadcast_to(jnp.exp(seg_end), (SUBLANES, LANES)), expand)[0:1, :]
        x = act_s[rows, 0:xw]
        xwt = (x * p_end).astype(BF16)
        for g in range(SSM_G):
            bg = act_s[rows, xw + g * SSM_N:xw + (g + 1) * SSM_N].astype(BF16)
            cg = act_s[rows, xw + (SSM_G + g) * SSM_N:xw + (SSM_G + g + 1) * SSM_N].astype(BF16)
            gmat = _dot_nt(cg, bg)
            st = st_ref[g]
            y_in = _dot(cg, st.astype(BF16)) * p_in[:, g * gw:(g + 1) * gw]
            for hh in range(SSM_HPG):
                h = g * SSM_HPG + hh
                col = SM_DT + h
                seg_c = seg[:, col:col + 1]
                seg_r = seg_t[col:col + 1, :]
                dt_r = dt_t[col:col + 1, :]
                lmat = jnp.exp(jnp.where(causal, seg_c - seg_r, -jnp.inf))
                scores = (gmat * lmat * dt_r).astype(BF16)
                xh = x[:, h * SSM_P:(h + 1) * SSM_P]
                yh = _dot(scores, xh.astype(BF16)) + y_in[:, hh * SSM_P:(hh + 1) * SSM_P]
                ys_s[rows, h * SSM_P:(h + 1) * SSM_P] = yh + dexp_ref[:, h * SSM_P:(h + 1) * SSM_P] * xh
            st_ref[g] = dec[:, g * gw:(g + 1) * gw] * st + _dot_tn(bg, xwt[:, g * gw:(g + 1) * gw])
        return carry

    for c in range(t_len // CHUNK):
        chunk(c, 0)
    y = ys_s[...] * _silu(z_ref[...])
    y_ref[...] = (_rms(y) * norm_ref[...]).astype(y_ref.dtype)


def _ssd(proj, conv_w, conv_b, dtb_row, alog_row, d_exp, norm, expand, t_len=256):
    bsz, s_len, _ = proj.shape
    xw = SSM_HEADS * SSM_P
    cw = xw + 2 * SSM_G * SSM_N
    return pl.pallas_call(
        _ssd_body,
        grid=(bsz, s_len // t_len),
        in_specs=[
            pl.BlockSpec((None, t_len, cw), lambda b, t: (b, t, PJ_XBC // cw)),
            pl.BlockSpec((None, t_len, xw), lambda b, t: (b, t, PJ_SZ // xw)),
            pl.BlockSpec((None, t_len, LANES), lambda b, t: (b, t, PJ_SMALL // LANES)),
            pl.BlockSpec((CONV_K, cw), lambda b, t: (0, 0)),
            pl.BlockSpec((1, cw), lambda b, t: (0, 0)),
            pl.BlockSpec((1, LANES), lambda b, t: (0, 0)),
            pl.BlockSpec((1, LANES), lambda b, t: (0, 0)),
            pl.BlockSpec((1, xw), lambda b, t: (0, 0)),
            pl.BlockSpec((1, xw), lambda b, t: (0, 0)),
            pl.BlockSpec((LANES, xw), lambda b, t: (0, 0)),
        ],
        out_specs=pl.BlockSpec((None, t_len, xw), lambda b, t: (b, t, 0)),
        out_shape=jax.ShapeDtypeStruct((bsz, s_len, xw), BF16),
        scratch_shapes=[
            pltpu.VMEM((SUBLANES, cw), F32),
            pltpu.VMEM((t_len + SUBLANES, cw), F32),
            pltpu.VMEM((t_len, cw), F32),
            pltpu.VMEM((t_len, LANES), F32),
            pltpu.VMEM((t_len, xw), F32),
            pltpu.VMEM((SSM_G, SSM_N, SSM_HPG * SSM_P), F32),
        ],
        compiler_params=_cparams(("parallel", "arbitrary")),
    )(proj, proj, proj, conv_w, conv_b.reshape(1, cw), dtb_row, alog_row, d_exp, norm.reshape(1, xw), expand)


def _dsa_prep_body(dq_ref, dkv_ref, iq_ref, sm_ref, qn_ref, kn_ref, kvn_ref, wuk_ref, wuv_ref, ikn_ref,
                   qbd_ref, kd_ref, vt_ref, qit_ref, kidx_ref, widx_ref):
    t_len = dq_ref.shape[0]
    dq = dq_ref[...]
    q_scale = (DSA_HD ** -0.5) * LOG2_E
    qd = jnp.concatenate(
        [_rms(dq[:, h * DSA_HD:(h + 1) * DSA_HD]) * qn_ref[...] * q_scale for h in range(DSA_HEADS)],
        axis=1)
    ckv = (_rms(dkv_ref[...]) * kvn_ref[...]).astype(BF16)
    kraw = _dot(ckv, wuk_ref[...])
    kd_ref[...] = jnp.concatenate(
        [_rms(kraw[:, h * DSA_HD:(h + 1) * DSA_HD]) * kn_ref[...] for h in range(DSA_HEADS)],
        axis=1).astype(BF16)
    v_t = _t_blocks_rows(_dot(ckv, wuv_ref[...])).astype(BF16)
    ones = jnp.ones((V_ROWS - DSA_HD, t_len), BF16)
    vt_ref[...] = jnp.concatenate(
        [blk for h in range(DSA_HEADS) for blk in (v_t[h * DSA_HD:(h + 1) * DSA_HD], ones)], axis=0)
    sm = sm_ref[...]
    kidx_ref[...] = (_rms(sm[:, SM_IK:SM_IK + IDX_D]) * ikn_ref[...]).astype(BF16)
    iq = iq_ref[...] * (IDX_D ** -0.5)
    wsc = sm * (IDX_HEADS ** -0.5)
    zero = jnp.zeros((DSA_HD, Q_BLOCK), BF16)
    for j in range(t_len // Q_BLOCK):
        rows = slice(j * Q_BLOCK, (j + 1) * Q_BLOCK)
        qt = _t_blocks(qd[rows]).astype(BF16)
        for p in range(DSA_HEADS // 2):
            top = jnp.concatenate([qt[2 * p * DSA_HD:(2 * p + 1) * DSA_HD], zero], axis=1)
            bot = jnp.concatenate([zero, qt[(2 * p + 1) * DSA_HD:(2 * p + 2) * DSA_HD]], axis=1)
            qbd_ref[j, p] = jnp.concatenate([top, bot], axis=0)
        it = _t_blocks(iq[rows]).astype(BF16)
        qit_ref[j] = jnp.concatenate([it[h * IDX_D:(h + 1) * IDX_D] for h in range(IDX_HEADS)], axis=1)
        wt = wsc[rows].T
        widx_ref[j] = jnp.concatenate([wt[SM_IW + h:SM_IW + h + 1] for h in range(IDX_HEADS)], axis=1)


def _t_blocks_rows(x):
    rows = x.shape[0]
    return jnp.concatenate([_t_blocks(x[r:r + LANES]) for r in range(0, rows, LANES)], axis=1)


def _dsa_prep(proj, q_norm, k_norm, kv_norm, w_uk, w_uv, idx_k_norm, t_len=256):
    bsz, s_len, _ = proj.shape
    nq = s_len // Q_BLOCK
    qpb = t_len // Q_BLOCK
    bw = DSA_HEADS * DSA_HD
    iw = IDX_HEADS * IDX_D
    npair = DSA_HEADS // 2
    return pl.pallas_call(
        _dsa_prep_body,
        grid=(bsz, s_len // t_len),
        in_specs=[
            pl.BlockSpec((None, t_len, bw), lambda b, t: (b, t, PJ_DQ // bw)),
            pl.BlockSpec((None, t_len, DSA_LAT), lambda b, t: (b, t, PJ_DKV // DSA_LAT)),
            pl.BlockSpec((None, t_len, iw), lambda b, t: (b, t, PJ_IQ // iw)),
            pl.BlockSpec((None, t_len, LANES), lambda b, t: (b, t, PJ_SMALL // LANES)),
            pl.BlockSpec((1, DSA_HD), lambda b, t: (0, 0)),
            pl.BlockSpec((1, DSA_HD), lambda b, t: (0, 0)),
            pl.BlockSpec((1, DSA_LAT), lambda b, t: (0, 0)),
            pl.BlockSpec((DSA_LAT, bw), lambda b, t: (0, 0)),
            pl.BlockSpec((DSA_LAT, bw), lambda b, t: (0, 0)),
            pl.BlockSpec((1, IDX_D), lambda b, t: (0, 0)),
        ],
        out_specs=[
            pl.BlockSpec((None, qpb, npair, 2 * DSA_HD, 2 * Q_BLOCK), lambda b, t: (b, t, 0, 0, 0)),
            pl.BlockSpec((None, t_len, bw), lambda b, t: (b, t, 0)),
            pl.BlockSpec((None, DSA_HEADS * V_ROWS, t_len), lambda b, t: (b, 0, t)),
            pl.BlockSpec((None, qpb, IDX_D, IDX_HEADS * Q_BLOCK), lambda b, t: (b, t, 0, 0)),
            pl.BlockSpec((None, t_len, IDX_D), lambda b, t: (b, t, 0)),
            pl.BlockSpec((None, qpb, 1, IDX_HEADS * Q_BLOCK), lambda b, t: (b, t, 0, 0)),
        ],
        out_shape=[
            jax.ShapeDtypeStruct((bsz, nq, npair, 2 * DSA_HD, 2 * Q_BLOCK), BF16),
            jax.ShapeDtypeStruct((bsz, s_len, bw), BF16),
            jax.ShapeDtypeStruct((bsz, DSA_HEADS * V_ROWS, s_len), BF16),
            jax.ShapeDtypeStruct((bsz, nq, IDX_D, IDX_HEADS * Q_BLOCK), BF16),
            jax.ShapeDtypeStruct((bsz, s_len, IDX_D), BF16),
            jax.ShapeDtypeStruct((bsz, nq, 1, IDX_HEADS * Q_BLOCK), F32),
        ],
        compiler_params=_cparams(("parallel", "parallel")),
    )(proj, proj, proj, proj, q_norm.reshape(1, DSA_HD), k_norm.reshape(1, DSA_HD),
      kv_norm.reshape(1, DSA_LAT), w_uk, w_uv, idx_k_norm.reshape(1, IDX_D))


def _dsa_body(qtab, ktab, ltab, kidx_ref, qit_ref, widx_ref, qbd_ref, kd_ref, vt_ref, y_ref,
              keys_ref, gmax_ref, thr_ref, cnt_ref, m_ref, acc_ref, *pair_refs, tk, top, sub_a, sub_b, sub_c):
    lg_refs = pair_refs[:DSA_HEADS // 2]
    pb_refs = pair_refs[DSA_HEADS // 2:]
    step = pl.program_id(1)
    qi = qtab[step]
    kj = ktab[step]
    n_keys_tiles = ltab[step] + 1
    pair_w = 2 * Q_BLOCK

    @pl.when(kj == 0)
    def _():
        lane = lax.broadcasted_iota(I32, (1, Q_BLOCK), 1)
        limit = qi * Q_BLOCK + (lane // CHUNK + 1) * CHUNK
        row = lax.broadcasted_iota(I32, (sub_a, Q_BLOCK), 0)

        def score_tile(t, carry):
            for c in range(tk // sub_a):
                r0 = pl.multiple_of(t * tk + c * sub_a, sub_a)
                kt = kidx_ref[pl.ds(r0, sub_a), :]
                acc2 = jnp.zeros((sub_a, pair_w), F32)
                for p in range(IDX_HEADS // 2):
                    cols = slice(p * pair_w, (p + 1) * pair_w)
                    acc2 = acc2 + jnp.maximum(_dot(kt, qit_ref[:, cols]), 0.0) * widx_ref[:, cols]
                acc = acc2[:, :Q_BLOCK] + acc2[:, Q_BLOCK:]
                score = jnp.where(row + r0 < limit, acc, -jnp.inf)
                bits = lax.bitcast_convert_type(score, I32)
                key = jnp.where(bits < 0, bits ^ jnp.int32(0x7FFFFFFF), bits)
                keys_ref[pl.ds(r0, sub_a), :] = key
                g0 = (c * sub_a) % N_GROUPS
                gmax_ref[g0:g0 + sub_a, :] = jnp.maximum(gmax_ref[g0:g0 + sub_a, :], key)
            return carry

        gmax_ref[...] = jnp.full_like(gmax_ref, INT_MIN)
        lax.fori_loop(0, n_keys_tiles, score_tile, 0)

        gm = gmax_ref[...]
        k_hi = jnp.max(gm, axis=0, keepdims=True)
        k_lo = jnp.min(gm, axis=0, keepdims=True)
        n_bits = jnp.int32(32) - jnp.min(lax.clz(k_hi ^ k_lo))
        low_mask = jnp.where(n_bits >= 32, jnp.int32(-1), lax.shift_left(jnp.int32(1), n_bits) - 1)
        t_init = ((k_hi ^ INT_MIN) & ~low_mask) ^ INT_MIN
        n_acc = 4 * SUBLANES
        n_b_tiles = n_keys_tiles * (tk // sub_b)

        def count_ge(cand):
            def count_tile(t, cnt):
                r0 = pl.multiple_of(t * sub_b, sub_b)
                ge = keys_ref[pl.ds(r0, sub_b), :] >= cand
                ones = jnp.where(ge, jnp.int32(1), jnp.int32(0))
                return cnt + jnp.sum(ones.reshape(sub_b // n_acc, n_acc, Q_BLOCK), axis=0)

            cnt = lax.fori_loop(0, n_b_tiles, count_tile, jnp.zeros((n_acc, Q_BLOCK), I32))
            return jnp.sum(cnt, axis=0, keepdims=True)

        def bit_step(i, state):
            thr, at_thr = state
            cand = thr + lax.shift_left(jnp.int32(1), n_bits - 1 - i)
            total = count_ge(cand)
            keep = total >= top
            return jnp.where(keep, cand, thr), jnp.where(keep, total, at_thr)

        n_head = n_bits % SEARCH_GROUP
        state = lax.fori_loop(0, n_head, bit_step, (t_init, jnp.full((1, Q_BLOCK), -1, I32)))

        def group_step(gs):
            g, st, _ = gs
            for b in range(SEARCH_GROUP):
                st = bit_step(n_head + g * SEARCH_GROUP + b, st)
            open_cnt = jnp.where(st[1] < 0, jnp.int32(2 ** 30), st[1])
            return g + 1, st, jnp.max(open_cnt)

        def more(gs):
            g, _, most = gs
            return jnp.logical_and(g * SEARCH_GROUP + n_head < n_bits, most > top)

        _, (thr, at_thr), _ = lax.while_loop(more, group_step, (jnp.int32(0), state, jnp.int32(2 ** 30)))
        thr_ref[...] = jnp.broadcast_to(thr, thr_ref.shape)
        cnt_ref[...] = jnp.broadcast_to(at_thr, cnt_ref.shape)

        @pl.when(jnp.min(at_thr) < 0)
        def _():
            t = thr_ref[0:1, :]
            cnt_ref[...] = jnp.broadcast_to(count_ge(t), cnt_ref.shape)

        surplus = jnp.where(thr > NEG_INF_KEY, cnt_ref[0:1, :], 0)

        @pl.when(jnp.max(surplus) > top)
        def _():
            t = thr_ref[0:1, :]
            need = top - count_ge(t + 1)

            row = lax.broadcasted_iota(I32, (SUBLANES, Q_BLOCK), 0)

            def fix_rows(g, seen):
                r0 = pl.multiple_of(g * SUBLANES, SUBLANES)
                k = keys_ref[pl.ds(r0, SUBLANES), :]
                tie = k == t
                run = jnp.where(tie, jnp.int32(1), jnp.int32(0))
                for sh in (1, 2, 4):
                    run = run + jnp.where(row >= sh, pltpu.roll(run, sh, 0), 0)
                keys_ref[pl.ds(r0, SUBLANES), :] = jnp.where(jnp.logical_and(tie, seen + run > need), k - 1, k)
                return seen + run[SUBLANES - 1:SUBLANES, :]

            lax.fori_loop(0, n_keys_tiles * (tk // SUBLANES), fix_rows, jnp.zeros((1, Q_BLOCK), I32))

        thr = jnp.maximum(thr, jnp.int32(NEG_INF_KEY + 1))
        thr_ref[...] = jnp.broadcast_to(thr, thr_ref.shape)
        m_ref[...] = jnp.full_like(m_ref, NEG_BIG)
        acc_ref[...] = jnp.zeros_like(acc_ref)

    k0 = pl.multiple_of(kj * tk, tk)
    npair = DSA_HEADS // 2
    thr = thr_ref[0:1, :]

    def logits(p):
        lg_refs[p][...] = _dot(kd_ref[:, p * 2 * DSA_HD:(p + 1) * 2 * DSA_HD], qbd_ref[p])

    def mask_pair(p):
        out = [jnp.full((SUBLANES, Q_BLOCK), NEG_BIG, F32) for _ in range(2)]
        for c in range(tk // sub_c):
            rows = slice(c * sub_c, (c + 1) * sub_c)
            sel = keys_ref[pl.ds(k0 + c * sub_c, sub_c), :] >= thr
            for j in range(2):
                qs = slice(j * Q_BLOCK, (j + 1) * Q_BLOCK)
                lg = jnp.where(sel, lg_refs[p][rows, qs], NEG_BIG)
                lg_refs[p][rows, qs] = lg
                out[j] = jnp.maximum(out[j], jnp.max(lg.reshape(sub_c // SUBLANES, SUBLANES, Q_BLOCK), axis=0))
        return out

    def values(p, mx):
        m_old = m_ref[p:p + 1, :]
        tile_max = jnp.concatenate([jnp.max(mx[j], axis=0, keepdims=True) for j in range(2)], axis=1)
        m_new = jnp.maximum(m_old, tile_max)
        alpha = jnp.exp2(m_old - m_new)
        m_ref[p:p + 1, :] = m_new
        pb_refs[p][...] = jnp.exp2(lg_refs[p][...] - m_new).astype(BF16)
        for j in range(2):
            h = 2 * p + j
            qs = slice(j * Q_BLOCK, (j + 1) * Q_BLOCK)
            acc_ref[h] = (alpha[:, qs] * acc_ref[h]
                          + _dot(vt_ref[h * V_ROWS:(h + 1) * V_ROWS, :], pb_refs[p][:, qs]))

    logits(0)
    logits(1)
    mx0 = mask_pair(0)
    logits(2)
    mx1 = mask_pair(1)
    values(0, mx0)
    logits(3)
    mx2 = mask_pair(2)
    values(1, mx1)
    mx3 = mask_pair(3)
    values(2, mx2)
    values(3, mx3)

    @pl.when(kj == ltab[step])
    def _():
        for h in range(DSA_HEADS):
            acc = acc_ref[h]
            o_t = acc[:DSA_HD] / acc[DSA_HD:DSA_HD + 1]
            y_ref[:, h * DSA_HD:(h + 1) * DSA_HD] = o_t.T.astype(y_ref.dtype)


def _dsa_tables(s_len, tk):
    qs, ks, ls = [], [], []
    for qi in range(s_len // Q_BLOCK):
        last = (qi * Q_BLOCK + Q_BLOCK - 1) // tk
        for kj in range(last + 1):
            qs.append(qi)
            ks.append(kj)
            ls.append(last)
    return (np.asarray(qs, np.int32), np.asarray(ks, np.int32), np.asarray(ls, np.int32))


def _dsa(qbd, kd, vt, qit, kidx, widx, tk=1024, sub_a=128, sub_b=1024, sub_c=128):
    bsz, s_len, bw = kd.shape
    npair = DSA_HEADS // 2
    tk = min(tk, s_len)
    top = min(INDEX_TOPK, s_len // 4)
    qtab, ktab, ltab = _dsa_tables(s_len, tk)
    grid_spec = pltpu.PrefetchScalarGridSpec(
        num_scalar_prefetch=3,
        grid=(bsz, len(qtab)),
        in_specs=[
            pl.BlockSpec((None, s_len, IDX_D), lambda b, s, qt, kt, lt: (b, 0, 0)),
            pl.BlockSpec((None, None, IDX_D, IDX_HEADS * Q_BLOCK), lambda b, s, qt, kt, lt: (b, qt[s], 0, 0)),
            pl.BlockSpec((None, None, 1, IDX_HEADS * Q_BLOCK), lambda b, s, qt, kt, lt: (b, qt[s], 0, 0)),
            pl.BlockSpec((None, None, npair, 2 * DSA_HD, 2 * Q_BLOCK), lambda b, s, qt, kt, lt: (b, qt[s], 0, 0, 0)),
            pl.BlockSpec((None, tk, bw), lambda b, s, qt, kt, lt: (b, kt[s], 0)),
            pl.BlockSpec((None, DSA_HEADS * V_ROWS, tk), lambda b, s, qt, kt, lt: (b, 0, kt[s])),
        ],
        out_specs=pl.BlockSpec((None, Q_BLOCK, bw), lambda b, s, qt, kt, lt: (b, qt[s], 0)),
        scratch_shapes=[
            pltpu.VMEM((s_len, Q_BLOCK), I32),
            pltpu.VMEM((N_GROUPS, Q_BLOCK), I32),
            pltpu.VMEM((SUBLANES, Q_BLOCK), I32),
            pltpu.VMEM((SUBLANES, Q_BLOCK), I32),
            pltpu.VMEM((SUBLANES, 2 * Q_BLOCK), F32),
            pltpu.VMEM((DSA_HEADS, V_ROWS, Q_BLOCK), F32),
        ] + [pltpu.VMEM((tk, 2 * Q_BLOCK), F32) for _ in range(npair)] + [
            pltpu.VMEM((tk, 2 * Q_BLOCK), BF16) for _ in range(npair)
        ],
    )
    return pl.pallas_call(
        functools.partial(_dsa_body, tk=tk, top=top, sub_a=sub_a, sub_b=min(sub_b, tk), sub_c=sub_c),
        grid_spec=grid_spec,
        out_shape=jax.ShapeDtypeStruct((bsz, s_len, bw), BF16),
        compiler_params=_cparams(("parallel", "arbitrary")),
    )(jnp.asarray(qtab), jnp.asarray(ktab), jnp.asarray(ltab), kidx, qit, widx, qbd, kd, vt)


def _merge_body(x_ref, g0_ref, g1_ref, g2_ref, ya_ref, yb_ref, yc_ref, wbr_ref, wout_ref, o_ref):
    @pl.when(pl.program_id(1) == 0)
    def _():
        o_ref[...] = x_ref[...]

    merged = (_sigmoid(g0_ref[...]) * _dot(ya_ref[...], wbr_ref[0])
              + _sigmoid(g1_ref[...]) * _dot(yb_ref[...], wbr_ref[1])
              + _sigmoid(g2_ref[...]) * _dot(yc_ref[...], wbr_ref[2]))
    o_ref[...] += _dot(merged.astype(BF16), wout_ref[...])


def _merge(x2, proj2, ya, yb, yc, w_branch, w_out, tm=512, tn=512):
    m, d = x2.shape
    bw = ya.shape[1]
    nj = d // tn
    gate_spec = lambda b: pl.BlockSpec((tm, tn), lambda i, j: (i, PJ_GATES // tn + b * nj + j))
    y_spec = pl.BlockSpec((tm, bw), lambda i, j: (i, 0))
    return pl.pallas_call(
        _merge_body,
        grid=(m // tm, nj),
        in_specs=[
            pl.BlockSpec((tm, d), lambda i, j: (i, 0)),
            gate_spec(0), gate_spec(1), gate_spec(2),
            y_spec, y_spec, y_spec,
            pl.BlockSpec((N_BRANCH, bw, tn), lambda i, j: (0, 0, j)),
            pl.BlockSpec((tn, d), lambda i, j: (j, 0)),
        ],
        out_specs=pl.BlockSpec((tm, d), lambda i, j: (i, 0)),
        out_shape=jax.ShapeDtypeStruct((m, d), F32),
        compiler_params=_cparams(("parallel", "arbitrary")),
    )(x2, proj2, proj2, proj2, ya, yb, yc, w_branch, w_out)


def _xattn_body(x_ref, g_ref, wq_ref, kv_ref, qn_ref, kn_ref, wo_ref, o_ref):
    x = x_ref[...]
    xn = (_rms(x) * g_ref[...]).astype(BF16)
    q = _dot(xn, wq_ref[...])
    kv = kv_ref[...]
    xw = XA_HEADS * XA_HD
    outs = []
    for h in range(XA_HEADS):
        hs = slice(h * XA_HD, (h + 1) * XA_HD)
        qh = (_rms(q[:, hs]) * qn_ref[...] * (XA_HD ** -0.5)).astype(BF16)
        kh = (_rms(kv[:, hs]) * kn_ref[...]).astype(BF16)
        vh = kv[:, xw + h * XA_HD:xw + (h + 1) * XA_HD].astype(BF16)
        s = _dot_nt(qh, kh)
        p = jnp.exp(s - jnp.max(s, axis=1, keepdims=True))
        p = p / jnp.sum(p, axis=1, keepdims=True)
        outs.append(_dot(p.astype(BF16), vh))
    o = jnp.concatenate(outs, axis=1).astype(BF16)
    o_ref[...] = x + _dot(o, wo_ref[...])


def _xattn(x, g, wq, kv, q_norm, k_norm, wo, tm=512):
    bsz, s_len, d = x.shape
    n_mem = kv.shape[1]
    xw = XA_HEADS * XA_HD
    return pl.pallas_call(
        _xattn_body,
        grid=(bsz, s_len // tm),
        in_specs=[
            pl.BlockSpec((None, tm, d), lambda b, i: (b, i, 0)),
            pl.BlockSpec((1, d), lambda b, i: (0, 0)),
            pl.BlockSpec((d, xw), lambda b, i: (0, 0)),
            pl.BlockSpec((None, n_mem, 2 * xw), lambda b, i: (b, 0, 0)),
            pl.BlockSpec((1, XA_HD), lambda b, i: (0, 0)),
            pl.BlockSpec((1, XA_HD), lambda b, i: (0, 0)),
            pl.BlockSpec((xw, d), lambda b, i: (0, 0)),
        ],
        out_specs=pl.BlockSpec((None, tm, d), lambda b, i: (b, i, 0)),
        out_shape=jax.ShapeDtypeStruct((bsz, s_len, d), F32),
        compiler_params=_cparams(("parallel", "parallel")),
    )(x, g.reshape(1, d), wq, kv, q_norm.reshape(1, XA_HD), k_norm.reshape(1, XA_HD), wo)


def _small_row(entries):
    row = jnp.zeros((1, LANES), F32)
    for off, vec in entries:
        row = row.at[0, off:off + vec.shape[0]].set(vec.astype(F32))
    return row


def _mixer(x2, bsz, mix_norm, w_in, ml_conv, ml_i_bias, ml_f_bias, ml_out_norm,
           dsa_q_norm, dsa_k_norm, dsa_kv_norm, dsa_w_uk, dsa_w_uv, idx_k_norm,
           ssm_conv, ssm_conv_b, ssm_dt_bias, ssm_a_log, ssm_d, ssm_norm, w_branch, w_out):
    m, d = x2.shape
    s_len = m // bsz
    proj2 = _proj(x2, mix_norm, _reorder_w_in(w_in), tn=1024)
    proj = proj2.reshape(bsz, s_len, PJ_WIDTH)

    ya = _mlstm(proj, ml_conv, _small_row([(SM_MI, ml_i_bias), (SM_MF, ml_f_bias)]), ml_out_norm)

    head_of_lane = np.arange(SSM_HEADS * SSM_P) // SSM_P
    expand = jnp.asarray(np.arange(LANES)[:, None] == (SM_DT + head_of_lane)[None, :], BF16)
    yc = _ssd(proj, ssm_conv, ssm_conv_b, _small_row([(SM_DT, ssm_dt_bias)]),
              _small_row([(SM_DT, ssm_a_log)]), jnp.repeat(ssm_d, SSM_P).reshape(1, -1), ssm_norm, expand)

    qbd, kd, vt, qit, kidx, widx = _dsa_prep(proj, dsa_q_norm, dsa_k_norm, dsa_kv_norm,
                                             dsa_w_uk.astype(BF16), dsa_w_uv.astype(BF16), idx_k_norm)
    yb = _dsa(qbd, kd, vt, qit, kidx, widx)

    bw = ya.shape[-1]
    return _merge(x2, proj2, ya.reshape(m, bw), yb.reshape(m, bw), yc.reshape(m, bw),
                  w_branch.astype(BF16), w_out.astype(BF16))


def kernel(x, mem, ffn1_norm, ffn1_w_up, ffn1_w_down, mix_norm, w_in, ml_conv, ml_i_bias, ml_f_bias, ml_out_norm, dsa_q_norm, dsa_k_norm, dsa_kv_norm, dsa_w_uk, dsa_w_uv, idx_k_norm, ssm_conv, ssm_conv_b, ssm_dt_bias, ssm_a_log, ssm_d, ssm_norm, w_branch, w_out, xa_norm, xa_mem_norm, xa_wq, xa_wkv, xa_q_norm, xa_k_norm, xa_wo, ffn2_norm, ffn2_w_up, ffn2_w_down):
    bsz, s_len, d = x.shape
    n_mem = mem.shape[1]
    depth = w_in.shape[0]
    x2 = x.reshape(bsz * s_len, d)
    mem2 = mem.reshape(bsz * n_mem, d)
    for l in range(depth):
        x2 = _ffn(x2, ffn1_norm[l], ffn1_w_up[l].astype(BF16), ffn1_w_down[l].astype(BF16))
        x2 = _mixer(x2, bsz, mix_norm[l], w_in[l], ml_conv[l], ml_i_bias[l], ml_f_bias[l], ml_out_norm[l],
                    dsa_q_norm[l], dsa_k_norm[l], dsa_kv_norm[l], dsa_w_uk[l], dsa_w_uv[l], idx_k_norm[l],
                    ssm_conv[l], ssm_conv_b[l], ssm_dt_bias[l], ssm_a_log[l], ssm_d[l], ssm_norm[l],
                    w_branch[l], w_out[l])
        kv = _proj(mem2, xa_mem_norm[l], xa_wkv[l].astype(BF16)).reshape(bsz, n_mem, -1)
        x3 = _xattn(x2.reshape(bsz, s_len, d), xa_norm[l], xa_wq[l].astype(BF16), kv,
                    xa_q_norm[l], xa_k_norm[l], xa_wo[l].astype(BF16))
        x2 = _ffn(x3.reshape(bsz * s_len, d), ffn2_norm[l], ffn2_w_up[l].astype(BF16), ffn2_w_down[l].astype(BF16))
    return x2.reshape(bsz, s_len, d)
```
